```python
import jax, jax.numpy as jnp
from jax import lax
import numpy as np

D_MODEL = 1024
BATCH = 2
SEQ = 8192
DEPTH = 1

CHUNK = 64
Q_BLOCK = 128
EPS = 1e-6

D_MIX = D_MODEL
SB_WIDTH = D_MIX // 2
SB_HEADS = 8
SB_HEAD_DIM = SB_WIDTH // SB_HEADS
GLA_WIDTH = D_MIX - SB_WIDTH
GLA_HEADS = 4
GLA_DV = GLA_WIDTH // GLA_HEADS
GLA_DK = GLA_DV // 2
GLA_KEY_WIDTH = GLA_HEADS * GLA_DK
GLA_GATE_RANK = 16
GLA_GATE_TAU = 16.0

IN_SIZES = (SB_WIDTH, SB_WIDTH, SB_WIDTH,
            GLA_KEY_WIDTH, GLA_KEY_WIDTH, GLA_WIDTH,
            GLA_GATE_RANK,
            GLA_WIDTH)
D_IN_PROJ = SB_WIDTH * 3 + GLA_KEY_WIDTH * 2 + GLA_WIDTH + GLA_GATE_RANK + GLA_WIDTH

PEER_HEADS = 8
PEER_N_KEYS = 128
PEER_N_EXPERTS = PEER_N_KEYS * PEER_N_KEYS
PEER_QUERY_DIM = 256
PEER_HALF = PEER_QUERY_DIM // 2
PEER_TOPK = 16
PEER_TOKEN_BLOCK = 128

kernel_name = "hybrid_sb_gla_peer_adaln_block"


def _rms(x):
    x32 = x.astype(jnp.float32)
    return (x32 * lax.rsqrt(jnp.mean(x32 * x32, axis=-1, keepdims=True) + EPS)).astype(x.dtype)


def _split_cols(t, sizes):
    outs, start = [], 0
    for s in sizes:
        outs.append(t[..., start:start + s])
        start += s
    return outs


def _heads(t, n_heads):
    b, s, _ = t.shape
    return t.reshape(b, s, n_heads, -1).transpose(0, 2, 1, 3)


def _merge_heads(t):
    b, n, s, d = t.shape
    return t.transpose(0, 2, 1, 3).reshape(b, s, n * d)


def stick_breaking_attention(q, k, v):
    b, h, s, dh = q.shape
    scale = dh ** -0.5
    n_blocks = s // Q_BLOCK
    key_pos = jnp.arange(s)

    def block(i):
        start = i * Q_BLOCK
        qb = lax.dynamic_slice_in_dim(q, start, Q_BLOCK, axis=2)
        z = jnp.einsum('bhtd,bhsd->bhts', qb, k).astype(jnp.float32) * scale
        q_pos = start + jnp.arange(Q_BLOCK)
        causal = key_pos[None, :] < q_pos[:, None]
        log_beta = jax.nn.log_sigmoid(z)
        log_keep = jnp.where(causal, jax.nn.log_sigmoid(-z), 0.0)
        suffix = lax.cumsum(log_keep, axis=3, reverse=True) - log_keep
        w = jnp.where(causal, jnp.exp(log_beta + suffix), 0.0)
        return jnp.einsum('bhts,bhsd->bhtd', w.astype(v.dtype), v)

    out = lax.map(block, jnp.arange(n_blocks))
    return out.transpose(1, 2, 0, 3, 4).reshape(b, h, s, dh)


def gla_chunked(q, k, v, log_a):
    b, h, s, dk = q.shape
    dv = v.shape[-1]
    n_chunks = s // CHUNK

    def to_chunks(t):
        return t.reshape(b, h, n_chunks, CHUNK, t.shape[-1]).transpose(2, 0, 1, 3, 4)

    qc, kc, vc, gc = (to_chunks(t) for t in (q * (dk ** -0.5), k, v, log_a))
    tri = jnp.tril(jnp.ones((CHUNK, CHUNK), dtype=bool))

    def step(state, inp):
        qi, ki, vi, gi = inp
        bcum = jnp.cumsum(gi.astype(jnp.float32), axis=2)
        b_last = bcum[:, :, -1:, :]
        diff = bcum[:, :, :, None, :] - bcum[:, :, None, :, :]
        decay = jnp.exp(jnp.where(tri[:, :, None], diff, -jnp.inf))
        scores = jnp.einsum('bhtd,bhsd,bhtsd->bhts', qi.astype(jnp.float32), ki.astype(jnp.float32), decay)
        o_intra = jnp.einsum('bhts,bhsv->bhtv', scores, vi.astype(jnp.float32))
        o_inter = jnp.einsum('bhtd,bhdv->bhtv', qi.astype(jnp.float32) * jnp.exp(bcum), state)
        k_dec = ki.astype(jnp.float32) * jnp.exp(b_last - bcum)
        new_state = state * jnp.exp(b_last[:, :, 0, :])[..., None] + jnp.einsum('bhsd,bhsv->bhdv', k_dec, vi.astype(jnp.float32))
        return new_state, o_intra + o_inter

    s0 = jnp.zeros((b, h, dk, dv), jnp.float32)
    _, out = lax.scan(step, s0, (qc, kc, vc, gc))
    return out.transpose(1, 2, 0, 3, 4).reshape(b, h, s, dv).astype(v.dtype)


def _head_rms(t, gain, n_heads):
    b, s, w = t.shape
    return _rms(t.reshape(b, s, n_heads, w // n_heads)).reshape(b, s, w) * gain


def peer_layer(h, w_q, keys_a, keys_b, expert_u, expert_v):
    t_tot, d = h.shape
    q = (h @ w_q).reshape(t_tot, PEER_HEADS, PEER_QUERY_DIM)
    q1, q2 = q[..., :PEER_HALF], q[..., PEER_HALF:]
    s1 = jnp.einsum('thd,hkd->thk', q1, keys_a)
    s2 = jnp.einsum('thd,hkd->thk', q2, keys_b)
    v1, i1 = lax.top_k(s1, PEER_TOPK)
    v2, i2 = lax.top_k(s2, PEER_TOPK)
    cand_s = (v1[..., :, None] + v2[..., None, :]).reshape(t_tot, PEER_HEADS, PEER_TOPK * PEER_TOPK)
    cand_i = (i1[..., :, None] * PEER_N_KEYS + i2[..., None, :]).reshape(t_tot, PEER_HEADS, PEER_TOPK * PEER_TOPK)
    top_s, pos = lax.top_k(cand_s, PEER_TOPK)
    idx = jnp.take_along_axis(cand_i, pos, axis=-1)
    gates = jax.nn.softmax(top_s.astype(jnp.float32), axis=-1).astype(h.dtype)

    n_blocks = t_tot // PEER_TOKEN_BLOCK

    def block(args):
        hb, ib, gb = args
        u = expert_u[ib]
        act = jax.nn.gelu(jnp.einsum('td,thkd->thk', hb, u), approximate=False)
        vv = expert_v[ib]
        return jnp.einsum('thk,thkd->td', gb * act, vv)

    out = lax.map(block, (h.reshape(n_blocks, PEER_TOKEN_BLOCK, d),
                          idx.reshape(n_blocks, PEER_TOKEN_BLOCK, PEER_HEADS, PEER_TOPK),
                          gates.reshape(n_blocks, PEER_TOKEN_BLOCK, PEER_HEADS, PEER_TOPK)))
    return out.reshape(t_tot, d)


def setup_inputs(seed: int = 0) -> dict:
    key = jax.random.key(seed)
    ks = jax.random.split(key, 16)
    d = D_MODEL
    nrm = jax.random.normal
    return {
        "x": nrm(ks[0], (BATCH, SEQ, d), jnp.float32),
        "c": nrm(ks[1], (BATCH, d), jnp.float32),
        "w_ada": nrm(ks[2], (DEPTH, d, 6 * d), jnp.float32) * (0.5 * d ** -0.5),
        "b_ada": nrm(ks[3], (DEPTH, 6 * d), jnp.float32) * 0.05,
        "w_in": nrm(ks[4], (DEPTH, d, D_IN_PROJ), jnp.float32) * d ** -0.5,
        "w_gla_gate_up": nrm(ks[5], (DEPTH, GLA_GATE_RANK, GLA_KEY_WIDTH), jnp.float32) * GLA_GATE_RANK ** -0.5,
        "b_gla_gate": nrm(ks[6], (DEPTH, GLA_KEY_WIDTH), jnp.float32) * 0.1,
        "g_sb_norm": 1.0 + 0.02 * nrm(ks[7], (DEPTH, SB_WIDTH), jnp.float32),
        "g_gla_norm": 1.0 + 0.02 * nrm(ks[8], (DEPTH, GLA_WIDTH), jnp.float32),
        "w_out": nrm(ks[9], (DEPTH, D_MIX, d), jnp.float32) * D_MIX ** -0.5,
        "w_peer_q": nrm(ks[10], (DEPTH, d, PEER_HEADS * PEER_QUERY_DIM), jnp.float32) * d ** -0.5,
        "peer_keys_a": nrm(ks[11], (DEPTH, PEER_HEADS, PEER_N_KEYS, PEER_HALF), jnp.float32) * PEER_HALF ** -0.5,
        "peer_keys_b": nrm(ks[12], (DEPTH, PEER_HEADS, PEER_N_KEYS, PEER_HALF), jnp.float32) * PEER_HALF ** -0.5,
        "peer_u": nrm(ks[13], (DEPTH, PEER_N_EXPERTS, d), jnp.float32) * d ** -0.5,
        "peer_v": nrm(ks[14], (DEPTH, PEER_N_EXPERTS, d), jnp.float32) * PEER_HEADS ** -0.5,
        "g_final": 1.0 + 0.02 * nrm(ks[15], (d,), jnp.float32),
    }


def reference(x, c, w_ada, b_ada, w_in, w_gla_gate_up, b_gla_gate, g_sb_norm, g_gla_norm,
              w_out, w_peer_q, peer_keys_a, peer_keys_b, peer_u, peer_v, g_final):
    b, s, d = x.shape
    c_act = jax.nn.silu(c)
    for l in range(DEPTH):
        mod = (c_act @ w_ada[l] + b_ada[l])[:, None, :]
        shift1, scale1, gate1, shift2, scale2, gate2 = _split_cols(mod, (d,) * 6)

        h = _rms(x) * (1.0 + scale1) + shift1
        proj = h @ w_in[l]
        sb_q, sb_k, sb_v, g_q, g_k, g_v, g_lr, g_r = _split_cols(proj, IN_SIZES)

        sb_out = stick_breaking_attention(_heads(sb_q, SB_HEADS), _heads(sb_k, SB_HEADS), _heads(sb_v, SB_HEADS))
        sb_out = _head_rms(_merge_heads(sb_out), g_sb_norm[l], SB_HEADS)

        log_a = jax.nn.log_sigmoid((g_lr @ w_gla_gate_up[l] + b_gla_gate[l]).astype(jnp.float32)) / GLA_GATE_TAU
        gla_out = gla_chunked(_heads(g_q, GLA_HEADS), _heads(g_k, GLA_HEADS), _heads(g_v, GLA_HEADS),
                              _heads(log_a, GLA_HEADS))
        gla_out = _head_rms(_merge_heads(gla_out), g_gla_norm[l], GLA_HEADS) * jax.nn.silu(g_r)

        mixed = jnp.concatenate([sb_out, gla_out], axis=-1) @ w_out[l]
        x = x + gate1 * mixed

        h2 = _rms(x) * (1.0 + scale2) + shift2
        y = peer_layer(h2.reshape(b * s, d), w_peer_q[l], peer_keys_a[l], peer_keys_b[l], peer_u[l], peer_v[l])
        x = x + gate2 * y.reshape(b, s, d)
    return _rms(x) * g_final
```

```python
import functools
import math

import jax
import jax.numpy as jnp
from jax import lax
from jax.experimental import pallas as pl
from jax.experimental.pallas import tpu as pltpu

F32 = jnp.float32
BF16 = jnp.bfloat16
HIGHEST = lax.Precision.HIGHEST

EPS = 1e-6
LANES = 128
SB_HEADS = 8
SB_HEAD_DIM = 64
GLA_HEADS = 4
GLA_DK = 64
GLA_DV = 128
GLA_GATE_RANK = 16
GLA_GATE_TAU = 16.0
GLA_CHUNK = 64
GLA_SUB = 16
GLA_EXP_CLAMP = 60.0
PEER_HEADS = 8
PEER_KEYS = 128
PEER_TOPK = 16
PEER_HALF = 128
SB_EXP_ZERO = -104.0
GS_PITCH = PEER_KEYS + 4

VMEM_LIMIT = 56 * 1024 * 1024


def _nt(a, b):
    return lax.dot_general(a, b, (((1,), (1,)), ((), ())), preferred_element_type=F32)


def _tn(a, b):
    return lax.dot_general(a, b, (((0,), (0,)), ((), ())), preferred_element_type=F32)


def _log_sigmoid(z):
    return jnp.minimum(z, 0.0) - jnp.log1p(jnp.exp(-jnp.abs(z)))


def _rms_scale(x):
    return lax.rsqrt(jnp.mean(x * x, axis=-1, keepdims=True) + EPS)


def _ada_kernel(c_ref, w_ref, b_ref, o_ref):
    c = c_ref[...]
    c_act = c * jax.nn.sigmoid(c)
    o_ref[...] = jnp.dot(c_act, w_ref[...], precision=HIGHEST, preferred_element_type=F32) + b_ref[...]


def _ada(c_pad, w, b, tn=1536):
    rows, d = c_pad.shape
    n = w.shape[1]
    return pl.pallas_call(
        _ada_kernel,
        grid=(n // tn,),
        in_specs=[pl.BlockSpec((rows, d), lambda j: (0, 0)),
                  pl.BlockSpec((d, tn), lambda j: (0, j)),
                  pl.BlockSpec((1, tn), lambda j: (0, j))],
        out_specs=pl.BlockSpec((rows, tn), lambda j: (0, j)),
        out_shape=jax.ShapeDtypeStruct((rows, n), F32),
        compiler_params=pltpu.CompilerParams(dimension_semantics=("arbitrary",), vmem_limit_bytes=VMEM_LIMIT),
        name="ada",
    )(c_pad, w, b)


def _inproj_kernel(x_ref, shift_ref, scale_ref, wm_ref, wlr_ref, wup_ref, bg_ref,
                   sbq_ref, sbk_ref, sbv_ref, gq_ref, gk_ref, gv_ref, gr_ref, la_ref):
    x = x_ref[0]
    h = (x * _rms_scale(x)) * (1.0 + scale_ref[0]) + shift_ref[0]
    hb = h.astype(BF16)
    sbw = SB_HEADS * SB_HEAD_DIM
    gkw = GLA_HEADS * GLA_DK
    gvw = GLA_HEADS * GLA_DV
    outs = ((sbq_ref, sbw, 0.125), (sbk_ref, sbw, None), (sbv_ref, sbw, None),
            (gq_ref, gkw, 0.125), (gk_ref, gkw, None), (gv_ref, gvw, None), (gr_ref, gvw, None))
    off = 0
    for ref, width, mul in outs:
        p = jnp.dot(hb, wm_ref[:, off:off + width], preferred_element_type=F32)
        if mul is not None:
            p = p * mul
        ref[0] = p.astype(ref.dtype)
        off += width
    g_lr = jnp.dot(hb, wlr_ref[...], preferred_element_type=F32)
    u = jnp.dot(g_lr, wup_ref[...], precision=HIGHEST, preferred_element_type=F32) + bg_ref[...]
    la_ref[0] = _log_sigmoid(u) * (1.0 / GLA_GATE_TAU)


def _inproj(x, shift, scale, w_main, w_lr, w_up, b_gate, tm=512):
    b, s, d = x.shape
    sbw = SB_HEADS * SB_HEAD_DIM
    gkw = GLA_HEADS * GLA_DK
    gvw = GLA_HEADS * GLA_DV
    widths = (sbw, sbw, sbw, gkw, gkw, gvw, gvw)
    tok = lambda w: pl.BlockSpec((1, tm, w), lambda bi, i: (bi, i, 0))
    full = lambda a: pl.BlockSpec(a.shape, lambda bi, i: (0,) * a.ndim)
    mod = pl.BlockSpec((1, 1, d), lambda bi, i: (bi, 0, 0))
    return pl.pallas_call(
        _inproj_kernel,
        grid=(b, s // tm),
        in_specs=[tok(d), mod, mod, full(w_main), full(w_lr), full(w_up), full(b_gate)],
        out_specs=[tok(w) for w in widths] + [tok(gkw)],
        out_shape=[jax.ShapeDtypeStruct((b, s, w), BF16) for w in widths] + [jax.ShapeDtypeStruct((b, s, gkw), F32)],
        compiler_params=pltpu.CompilerParams(dimension_semantics=("arbitrary", "arbitrary"), vmem_limit_bytes=VMEM_LIMIT),
        name="inproj",
    )(x, shift, scale, w_main, w_lr, w_up, b_gate)


def _sb_kernel(q_ref, k_ref, v_ref, g_ref, o_ref, *, tq):
    i = pl.program_id(2)
    q = q_ref[0]
    lane = lax.broadcasted_iota(jnp.int32, (tq, LANES), 1)
    row = lax.broadcasted_iota(jnp.int32, (tq, tq), 0)
    col = lax.broadcasted_iota(jnp.int32, (tq, tq), 1)
    later = jnp.where(row > col, 1.0, 0.0).astype(BF16)
    causal = col < row

    def block(qh, j, r_run, acc, masked):
        start = pl.multiple_of(j * tq, tq)
        kb = k_ref[0, pl.ds(start, tq), :]
        vb = v_ref[0, pl.ds(start, tq), :]
        z = _nt(qh, kb)
        log_beta = _log_sigmoid(z)
        log_keep = log_beta - z
        if masked:
            log_keep = jnp.where(causal, log_keep, 0.0)
        hi = log_keep.astype(BF16)
        lo = (log_keep - hi.astype(F32)).astype(BF16)
        cs = jnp.dot(jnp.concatenate([hi, lo], axis=0), later, preferred_element_type=F32)
        suffix = cs[:tq] + cs[tq:] + r_run
        w = jnp.exp(log_beta + suffix)
        if masked:
            w = jnp.where(causal, w, 0.0)
        acc = acc + jnp.dot(w.astype(BF16), vb, preferred_element_type=F32)
        r_run = r_run + jnp.sum(log_keep, axis=-1, keepdims=True)
        return r_run, acc

    outs = []
    for h in range(LANES // SB_HEAD_DIM):
        in_head = (lane >= h * SB_HEAD_DIM) & (lane < (h + 1) * SB_HEAD_DIM)
        qh = jnp.where(in_head, q, jnp.zeros_like(q))
        r0, acc0 = block(qh, i, jnp.zeros((tq, 1), F32), jnp.zeros((tq, LANES), F32), True)

        def cond(c):
            j, r_run, _ = c
            return jnp.logical_and(j >= 0, jnp.max(r_run) > SB_EXP_ZERO)

        def body(c, qh=qh):
            j, r_run, acc = c
            r_run, acc = block(qh, j, r_run, acc, False)
            return j - 1, r_run, acc

        _, _, acc = lax.while_loop(cond, body, (i - 1, r0, acc0))
        outs.append((in_head, acc))

    o = jnp.where(outs[0][0], outs[0][1], outs[1][1])
    o2 = o * o
    ms = jnp.zeros_like(o)
    for in_head, _ in outs:
        ms_h = jnp.sum(jnp.where(in_head, o2, 0.0), axis=-1, keepdims=True) * (1.0 / SB_HEAD_DIM)
        ms = jnp.where(in_head, ms_h, ms)
    o_ref[0] = (o * lax.rsqrt(ms + EPS) * g_ref[...]).astype(o_ref.dtype)


def _sb_attention(q, k, v, gain, tq=128):
    b, s, w = q.shape
    pairs = w // LANES
    return pl.pallas_call(
        functools.partial(_sb_kernel, tq=tq),
        grid=(b, pairs, s // tq),
        in_specs=[pl.BlockSpec((1, tq, LANES), lambda bi, p, i: (bi, i, p)),
                  pl.BlockSpec((1, s, LANES), lambda bi, p, i: (bi, 0, p)),
                  pl.BlockSpec((1, s, LANES), lambda bi, p, i: (bi, 0, p)),
                  pl.BlockSpec((1, LANES), lambda bi, p, i: (0, p))],
        out_specs=pl.BlockSpec((1, tq, LANES), lambda bi, p, i: (bi, i, p)),
        out_shape=jax.ShapeDtypeStruct((b, s, w), BF16),
        compiler_params=pltpu.CompilerParams(dimension_semantics=("arbitrary",) * 3, vmem_limit_bytes=VMEM_LIMIT),
        name="sb",
    )(q, k, v, gain)


def _gla_kernel(q_ref, k_ref, v_ref, la_ref, r_ref, g_ref, o_ref, st_ref):
    c = GLA_CHUNK
    nsub = c // GLA_SUB

    @pl.when(pl.program_id(1) == 0)
    def _():
        st_ref[...] = jnp.zeros_like(st_ref)

    la = la_ref[0]
    row = lax.broadcasted_iota(jnp.int32, (c, c), 0)
    col = lax.broadcasted_iota(jnp.int32, (c, c), 1)
    incl = jnp.where(col <= row, 1.0, 0.0).astype(F32)
    g_cum = jnp.dot(incl, la, precision=HIGHEST, preferred_element_type=F32)
    g_exc = g_cum - la
    g_last = g_cum[c - 1:c, :]
    refs = [g_exc[GLA_SUB * sb:GLA_SUB * sb + 1, :] for sb in range(nsub)]
    r_rows = jnp.concatenate([jnp.broadcast_to(r, (GLA_SUB, la.shape[1])) for r in refs], axis=0)
    e_q = jnp.exp(g_cum - r_rows)
    e_inter = jnp.exp(g_cum)
    e_k_last = jnp.exp(g_last - g_cum)
    e_last = jnp.exp(g_last)
    causal = col <= row
    rowk = lax.broadcasted_iota(jnp.int32, (c, GLA_DK), 0)

    q_all = q_ref[0].astype(F32)
    k_all = k_ref[0].astype(F32)
    v_all = v_ref[0]
    gate = r_ref[0].astype(F32)
    gate = gate * jax.nn.sigmoid(gate)
    for h in range(GLA_HEADS):
        ks = slice(h * GLA_DK, (h + 1) * GLA_DK)
        vs = slice(h * GLA_DV, (h + 1) * GLA_DV)
        q = q_all[:, ks]
        k = k_all[:, ks]
        v = v_all[:, vs]
        gh = g_cum[:, ks]
        qt = q * e_q[:, ks]
        q_hat = jnp.concatenate(
            [jnp.where((rowk >= GLA_SUB * sb) & (rowk < GLA_SUB * (sb + 1)), qt, 0.0) for sb in range(nsub)],
            axis=1).astype(BF16)
        k_hat = jnp.concatenate(
            [jnp.where(rowk < GLA_SUB * (sb + 1),
                       k * jnp.exp(jnp.minimum(refs[sb][:, ks] - gh, GLA_EXP_CLAMP)), 0.0) for sb in range(nsub)],
            axis=1).astype(BF16)
        scores = jnp.where(causal, _nt(q_hat, k_hat), 0.0)
        o_intra = jnp.dot(scores.astype(BF16), v, preferred_element_type=F32)
        st = st_ref[h]
        o_inter = _nt((q * e_inter[:, ks]).astype(BF16), st.astype(BF16))
        k_dec = (k * e_k_last[:, ks]).astype(BF16)
        st_ref[h] = st * e_last[:, ks] + _tn(v, k_dec)
        o = o_intra + o_inter
        o = o * _rms_scale(o) * g_ref[:, vs] * gate[:, vs]
        o_ref[0, :, vs] = o.astype(o_ref.dtype)


def _gla(q, k, v, la, r, gain):
    b, s, _ = q.shape
    c = GLA_CHUNK
    kw = GLA_HEADS * GLA_DK
    vw = GLA_HEADS * GLA_DV
    tok = lambda w: pl.BlockSpec((1, c, w), lambda bi, i: (bi, i, 0))
    return pl.pallas_call(
        _gla_kernel,
        grid=(b, s // c),
        in_specs=[tok(kw), tok(kw), tok(vw), tok(kw), tok(vw), pl.BlockSpec((1, vw), lambda bi, i: (0, 0))],
        out_specs=tok(vw),
        out_shape=jax.ShapeDtypeStruct((b, s, vw), BF16),
        scratch_shapes=[pltpu.VMEM((GLA_HEADS, GLA_DV, GLA_DK), F32)],
        compiler_params=pltpu.CompilerParams(dimension_semantics=("arbitrary", "arbitrary"), vmem_limit_bytes=VMEM_LIMIT),
        name="gla",
    )(q, k, v, la, r, gain)


def _mix_kernel(sb_ref, gla_ref, x_ref, gate_ref, shift_ref, scale_ref, wo_ref, wq_ref, ka_ref, kb_ref,
                x1_ref, h2_ref, st_ref):
    half = sb_ref.shape[2]
    mixed = (jnp.dot(sb_ref[0], wo_ref[:half, :], preferred_element_type=F32)
             + jnp.dot(gla_ref[0], wo_ref[half:, :], preferred_element_type=F32))
    x1 = x_ref[0] + gate_ref[0] * mixed
    x1_ref[0] = x1
    h2 = ((x1 * _rms_scale(x1)) * (1.0 + scale_ref[0]) + shift_ref[0]).astype(BF16)
    h2_ref[0] = h2
    for h in range(PEER_HEADS):
        base = 2 * PEER_HALF * h
        q1 = jnp.dot(h2, wq_ref[:, base:base + PEER_HALF], preferred_element_type=F32).astype(BF16)
        q2 = jnp.dot(h2, wq_ref[:, base + PEER_HALF:base + 2 * PEER_HALF], preferred_element_type=F32).astype(BF16)
        st_ref[2 * h] = _nt(ka_ref[h], q1)
        st_ref[2 * h + 1] = _nt(kb_ref[h], q2)


def _mix(sb, gla, x, gate, shift, scale, w_out, w_q, keys_a, keys_b, tm=256):
    b, s, d = x.shape
    ns = s // tm
    tok = lambda w: pl.BlockSpec((1, tm, w), lambda bi, i: (bi, i, 0))
    full = lambda a: pl.BlockSpec(a.shape, lambda bi, i: (0,) * a.ndim)
    mod = pl.BlockSpec((1, 1, d), lambda bi, i: (bi, 0, 0))
    return pl.pallas_call(
        _mix_kernel,
        grid=(b, ns),
        in_specs=[tok(sb.shape[2]), tok(gla.shape[2]), tok(d), mod, mod, mod,
                  full(w_out), full(w_q), full(keys_a), full(keys_b)],
        out_specs=[tok(d), tok(d),
                   pl.BlockSpec((2 * PEER_HEADS, PEER_KEYS, tm), lambda bi, i: (0, 0, bi * ns + i))],
        out_shape=[jax.ShapeDtypeStruct((b, s, d), F32), jax.ShapeDtypeStruct((b, s, d), BF16),
                   jax.ShapeDtypeStruct((2 * PEER_HEADS, PEER_KEYS, b * s), F32)],
        compiler_params=pltpu.CompilerParams(dimension_semantics=("arbitrary", "arbitrary"), vmem_limit_bytes=VMEM_LIMIT),
        name="mix",
    )(sb, gla, x, gate, shift, scale, w_out, w_q, keys_a, keys_b)


def _extract_max(s, iota, n):
    m = jnp.max(s, axis=0, keepdims=True)
    idx = jnp.min(jnp.where(s == m, iota, n), axis=0, keepdims=True)
    hit = iota == idx
    return m, idx, hit, jnp.where(hit, -jnp.inf, s)


_PAIR_COUNTS = tuple(PEER_TOPK // (a + 1) for a in range(PEER_TOPK))
_N_CAND = sum(_PAIR_COUNTS)
_N_CAND_PAD = -(-_N_CAND // 8) * 8


def _topk_kernel(s_ref, i_ref, j_ref, g_ref, cand_ref, ci_ref, cj_ref, oi_ref, oj_ref, og_ref):
    tl = s_ref.shape[2]
    k = PEER_TOPK
    iota_keys = lax.broadcasted_iota(jnp.int32, (PEER_KEYS, tl), 0)
    iota_cand = lax.broadcasted_iota(jnp.int32, (_N_CAND_PAD, tl), 0)

    def top_list(s):
        vals, idxs = [], []
        for _ in range(k):
            m, idx, _, s = _extract_max(s, iota_keys, PEER_KEYS)
            vals.append(m)
            idxs.append(idx)
        return vals, idxs

    def head(h, carry):
        v1, i1 = top_list(s_ref[2 * h])
        v2, i2 = top_list(s_ref[2 * h + 1])
        v2_all = jnp.concatenate(v2, axis=0)
        i2_all = jnp.concatenate(i2, axis=0)
        cand_ref[...] = jnp.full((_N_CAND_PAD, tl), -jnp.inf, F32)
        ci_ref[...] = jnp.zeros((_N_CAND_PAD, tl), jnp.int32)
        cj_ref[...] = jnp.zeros((_N_CAND_PAD, tl), jnp.int32)
        off = 0
        for a, nb in enumerate(_PAIR_COUNTS):
            cand_ref[off:off + nb, :] = v1[a] + v2_all[:nb]
            ci_ref[off:off + nb, :] = jnp.broadcast_to(i1[a], (nb, tl))
            cj_ref[off:off + nb, :] = i2_all[:nb]
            off += nb
        cand = cand_ref[...]
        ci = ci_ref[...]
        cj = cj_ref[...]
        tops, sel_i, sel_j = [], [], []
        for _ in range(k):
            m, _, hit, cand = _extract_max(cand, iota_cand, _N_CAND_PAD)
            tops.append(m)
            sel_i.append(jnp.sum(jnp.where(hit, ci, 0), axis=0, keepdims=True))
            sel_j.append(jnp.sum(jnp.where(hit, cj, 0), axis=0, keepdims=True))
        top = jnp.concatenate(tops, axis=0)
        e = jnp.exp(top - top[0:1])
        gates = e / jnp.sum(e, axis=0, keepdims=True)
        rows = pl.ds(pl.multiple_of(h * k, k), k)
        oi_ref[rows, :] = jnp.concatenate(sel_i, axis=0)
        oj_ref[rows, :] = jnp.concatenate(sel_j, axis=0)
        og_ref[rows, :] = gates
        return carry

    lax.fori_loop(0, PEER_HEADS, head, 0)
    for t0 in range(0, tl, LANES):
        i_ref[t0:t0 + LANES, :] = oi_ref[:, t0:t0 + LANES].T
        j_ref[t0:t0 + LANES, :] = oj_ref[:, t0:t0 + LANES].T
        g_ref[t0:t0 + LANES, :] = og_ref[:, t0:t0 + LANES].T


def _topk(scores_t, tl=128):
    nh2, nk, t = scores_t.shape
    nsel = PEER_HEADS * PEER_TOPK
    out = pl.BlockSpec((tl, nsel), lambda i: (i, 0))
    return pl.pallas_call(
        _topk_kernel,
        grid=(t // tl,),
        in_specs=[pl.BlockSpec((nh2, nk, tl), lambda i: (0, 0, i))],
        out_specs=[out, out, out],
        out_shape=[jax.ShapeDtypeStruct((t, nsel), jnp.int32), jax.ShapeDtypeStruct((t, nsel), jnp.int32),
                   jax.ShapeDtypeStruct((t, nsel), F32)],
        scratch_shapes=[pltpu.VMEM((_N_CAND_PAD, tl), F32), pltpu.VMEM((_N_CAND_PAD, tl), jnp.int32),
                        pltpu.VMEM((_N_CAND_PAD, tl), jnp.int32),
                        pltpu.VMEM((nsel, tl), jnp.int32), pltpu.VMEM((nsel, tl), jnp.int32), pltpu.VMEM((nsel, tl), F32)],
        compiler_params=pltpu.CompilerParams(dimension_semantics=("arbitrary",), vmem_limit_bytes=VMEM_LIMIT),
        name="topk",
    )(scores_t)


def _peer_kernel(h2_ref, u_ref, v_ref, isel_ref, jsel_ref, gsel_ref, x1_ref, gate_ref, gf_ref, o_ref,
                 gs_ref, acc_ref, *, tm, te):
    e = pl.program_id(1)
    n_i = te // PEER_KEYS

    @pl.when(e == 0)
    def _():
        acc_ref[...] = jnp.zeros_like(acc_ref)
        iota0 = lax.broadcasted_iota(jnp.int32, (PEER_KEYS, LANES), 0)

        def per_token(t, carry):
            isel = isel_ref[pl.ds(t, 1), :]
            jsel = jsel_ref[pl.ds(t, 1), :]
            g = gsel_ref[pl.ds(t, 1), :]
            gi = jnp.where(iota0 == isel, g, 0.0).astype(BF16)
            oj = jnp.where(iota0 == jsel, 1.0, 0.0).astype(BF16)
            gs_ref[pl.ds(t * GS_PITCH, PEER_KEYS), :] = _nt(gi, oj)
            return carry

        lax.fori_loop(0, tm, per_token, 0)

    a = _nt(h2_ref[...], u_ref[...])
    act = 0.5 * a * (1.0 + lax.erf(a * (1.0 / math.sqrt(2.0))))
    parts = []
    for ii in range(n_i):
        gm = gs_ref[pl.ds(e * n_i + ii, tm, stride=GS_PITCH), :]
        parts.append((act[:, ii * PEER_KEYS:(ii + 1) * PEER_KEYS] * gm).astype(BF16))
    w = jnp.concatenate(parts, axis=1)
    acc_ref[...] += jnp.dot(w, v_ref[...], preferred_element_type=F32)

    @pl.when(e == pl.num_programs(1) - 1)
    def _():
        x2 = x1_ref[...] + gate_ref[0] * acc_ref[...]
        o_ref[...] = x2 * _rms_scale(x2) * gf_ref[...]


def _peer(h2, u, v, isel, jsel, gsel, x1, gate, g_final, seq, tm=256, te=512):
    t, d = h2.shape
    n_e = u.shape[0]
    per_batch = seq // tm
    tok = lambda w: pl.BlockSpec((tm, w), lambda i, e: (i, 0))
    exp = pl.BlockSpec((te, d), lambda i, e: (e, 0))
    return pl.pallas_call(
        functools.partial(_peer_kernel, tm=tm, te=te),
        grid=(t // tm, n_e // te),
        in_specs=[tok(d), exp, exp, tok(isel.shape[1]), tok(isel.shape[1]), tok(isel.shape[1]), tok(d),
                  pl.BlockSpec((1, 1, d), lambda i, e: (i // per_batch, 0, 0)),
                  pl.BlockSpec((1, d), lambda i, e: (0, 0))],
        out_specs=tok(d),
        out_shape=jax.ShapeDtypeStruct((t, d), F32),
        scratch_shapes=[pltpu.VMEM((tm * GS_PITCH, LANES), F32), pltpu.VMEM((tm, d), F32)],
        compiler_params=pltpu.CompilerParams(dimension_semantics=("arbitrary", "arbitrary"), vmem_limit_bytes=VMEM_LIMIT),
        name="peer",
    )(h2, u, v, isel, jsel, gsel, x1, gate, g_final)


def kernel(x, c, w_ada, b_ada, w_in, w_gla_gate_up, b_gla_gate, g_sb_norm, g_gla_norm, w_out, w_peer_q,
           peer_keys_a, peer_keys_b, peer_u, peer_v, g_final):
    b, s, d = x.shape
    depth = w_ada.shape[0]
    sbw = SB_HEADS * SB_HEAD_DIM
    gkw = GLA_HEADS * GLA_DK
    gvw = GLA_HEADS * GLA_DV
    lr0 = 3 * sbw + 2 * gkw + gvw
    c_pad = jnp.pad(c, ((0, 8 - b), (0, 0)))
    for l in range(depth):
        mod = _ada(c_pad, w_ada[l], b_ada[l][None, :])[:b]
        shift1, scale1, gate1, shift2, scale2, gate2 = (mod[:, None, i * d:(i + 1) * d] for i in range(6))

        w_l = w_in[l]
        w_main = jnp.concatenate([w_l[:, :lr0], w_l[:, lr0 + GLA_GATE_RANK:]], axis=1).astype(BF16)
        w_lr = jnp.pad(w_l[:, lr0:lr0 + GLA_GATE_RANK], ((0, 0), (0, LANES - GLA_GATE_RANK))).astype(BF16)
        w_up = jnp.pad(w_gla_gate_up[l], ((0, LANES - GLA_GATE_RANK), (0, 0)))
        sbq, sbk, sbv, gq, gk, gv, gr, la = _inproj(x, shift1, scale1, w_main, w_lr, w_up, b_gla_gate[l][None, :])

        sb_out = _sb_attention(sbq, sbk, sbv, g_sb_norm[l][None, :])
        gla_out = _gla(gq, gk, gv, la, gr, g_gla_norm[l][None, :])

        x1, h2, scores_t = _mix(sb_out, gla_out, x, gate1, shift2, scale2, w_out[l].astype(BF16),
                                w_peer_q[l].astype(BF16), peer_keys_a[l].astype(BF16), peer_keys_b[l].astype(BF16))
        isel, jsel, gsel = _topk(scores_t)
        x = _peer(h2.reshape(b * s, d), peer_u[l].astype(BF16), peer_v[l].astype(BF16), isel, jsel, gsel,
                  x1.reshape(b * s, d), gate2, g_final[None, :], s).reshape(b, s, d)
    return x
```

```python
import functools
import math

import jax
import jax.numpy as jnp
from jax import lax
from jax.experimental import pallas as pl
from jax.experimental.pallas import tpu as pltpu

F32 = jnp.float32
BF16 = jnp.bfloat16
HIGHEST = lax.Precision.HIGHEST

EPS = 1e-6
LANES = 128
SB_HEADS = 8
SB_HEAD_DIM = 64
GLA_HEADS = 4
GLA_DK = 64
GLA_DV = 128
GLA_GATE_RANK = 16
GLA_GATE_TAU = 16.0
GLA_CHUNK = 64
GLA_SUB = 16
GLA_EXP_CLAMP = 60.0
PEER_HEADS = 8
PEER_KEYS = 128
PEER_TOPK = 16
PEER_HALF = 128
SB_EXP_ZERO = -104.0
SB_SPAN = 3
GS_PITCH = PEER_KEYS // 2 + 4

VMEM_LIMIT = 56 * 1024 * 1024


def _nt(a, b):
    return lax.dot_general(a, b, (((1,), (1,)), ((), ())), preferred_element_type=F32)


def _tn(a, b):
    return lax.dot_general(a, b, (((0,), (0,)), ((), ())), preferred_element_type=F32)


def _log_sigmoid(z):
    return jnp.minimum(z, 0.0) - jnp.log1p(jnp.exp(-jnp.abs(z)))


def _rms_scale(x):
    return lax.rsqrt(jnp.mean(x * x, axis=-1, keepdims=True) + EPS)


def _ada_kernel(c_ref, w_ref, b_ref, o_ref):
    c = c_ref[...]
    c_act = c * jax.nn.sigmoid(c)
    o_ref[...] = jnp.dot(c_act, w_ref[...], precision=HIGHEST, preferred_element_type=F32) + b_ref[...]


def _ada(c_pad, w, b, tn=1536):
    rows, d = c_pad.shape
    n = w.shape[1]
    return pl.pallas_call(
        _ada_kernel,
        grid=(n // tn,),
        in_specs=[pl.BlockSpec((rows, d), lambda j: (0, 0)),
                  pl.BlockSpec((d, tn), lambda j: (0, j)),
                  pl.BlockSpec((1, tn), lambda j: (0, j))],
        out_specs=pl.BlockSpec((rows, tn), lambda j: (0, j)),
        out_shape=jax.ShapeDtypeStruct((rows, n), F32),
        compiler_params=pltpu.CompilerParams(dimension_semantics=("arbitrary",), vmem_limit_bytes=VMEM_LIMIT),
        name="ada",
    )(c_pad, w, b)


def _inproj_kernel(x_ref, shift_ref, scale_ref, wm_ref, wlr_ref, wup_ref, bg_ref,
                   sbq_ref, sbk_ref, sbv_ref, gq_ref, gk_ref, gv_ref, gr_ref, la_ref):
    x = x_ref[0]
    h = (x * _rms_scale(x)) * (1.0 + scale_ref[0]) + shift_ref[0]
    hb = h.astype(BF16)
    sbw = SB_HEADS * SB_HEAD_DIM
    gkw = GLA_HEADS * GLA_DK
    gvw = GLA_HEADS * GLA_DV
    outs = ((sbq_ref, sbw, 0.125), (sbk_ref, sbw, None), (sbv_ref, sbw, None),
            (gq_ref, gkw, 0.125), (gk_ref, gkw, None), (gv_ref, gvw, None), (gr_ref, gvw, None))
    off = 0
    for ref, width, mul in outs:
        p = jnp.dot(hb, wm_ref[:, off:off + width], preferred_element_type=F32)
        if mul is not None:
            p = p * mul
        ref[0] = p.astype(ref.dtype)
        off += width
    g_lr = jnp.dot(hb, wlr_ref[...], preferred_element_type=F32)
    u = jnp.dot(g_lr, wup_ref[...], precision=HIGHEST, preferred_element_type=F32) + bg_ref[...]
    la_ref[0] = _log_sigmoid(u) * (1.0 / GLA_GATE_TAU)


def _inproj(x, shift, scale, w_main, w_lr, w_up, b_gate, tm=512):
    b, s, d = x.shape
    sbw = SB_HEADS * SB_HEAD_DIM
    gkw = GLA_HEADS * GLA_DK
    gvw = GLA_HEADS * GLA_DV
    widths = (sbw, sbw, sbw, gkw, gkw, gvw, gvw)
    tok = lambda w: pl.BlockSpec((1, tm, w), lambda bi, i: (bi, i, 0))
    full = lambda a: pl.BlockSpec(a.shape, lambda bi, i: (0,) * a.ndim)
    mod = pl.BlockSpec((1, 1, d), lambda bi, i: (bi, 0, 0))
    return pl.pallas_call(
        _inproj_kernel,
        grid=(b, s // tm),
        in_specs=[tok(d), mod, mod, full(w_main), full(w_lr), full(w_up), full(b_gate)],
        out_specs=[tok(w) for w in widths] + [tok(gkw)],
        out_shape=[jax.ShapeDtypeStruct((b, s, w), BF16) for w in widths] + [jax.ShapeDtypeStruct((b, s, gkw), F32)],
        compiler_params=pltpu.CompilerParams(dimension_semantics=("arbitrary", "arbitrary"), vmem_limit_bytes=VMEM_LIMIT),
        name="inproj",
    )(x, shift, scale, w_main, w_lr, w_up, b_gate)


def _sb_kernel(q_ref, k_ref, v_ref, g_ref, o_ref, *, tq, nq):
    gi = pl.program_id(2)
    lane = lax.broadcasted_iota(jnp.int32, (tq, LANES), 1)
    row = lax.broadcasted_iota(jnp.int32, (tq, tq), 0)
    col = lax.broadcasted_iota(jnp.int32, (tq, tq), 1)
    later = jnp.where(row > col, 1.0, 0.0).astype(BF16)

    n_heads = LANES // SB_HEAD_DIM
    rows = n_heads * tq
    in_head = [(lane >= h * SB_HEAD_DIM) & (lane < (h + 1) * SB_HEAD_DIM) for h in range(n_heads)]
    span = SB_SPAN * tq
    q_row = lax.broadcasted_iota(jnp.int32, (rows, span), 0) & (tq - 1)
    k_col = lax.broadcasted_iota(jnp.int32, (rows, span), 1)

    def suffix_in_blocks(log_keep, nblk):
        parts = []
        for b in range(nblk):
            lk = log_keep[:, b * tq:(b + 1) * tq]
            hi = lk.astype(BF16)
            parts += [hi, (lk - hi.astype(F32)).astype(BF16)]
        cs = jnp.dot(jnp.concatenate(parts, axis=0), later, preferred_element_type=F32)
        return [cs[2 * b * rows:(2 * b + 1) * rows] + cs[(2 * b + 1) * rows:(2 * b + 2) * rows] for b in range(nblk)]

    def first_span(q2, i):
        b0 = jnp.maximum(i - (SB_SPAN - 1), 0)
        start = pl.multiple_of(b0 * tq, tq)
        z = _nt(q2, k_ref[0, pl.ds(start, span), :])
        log_beta = _log_sigmoid(z)
        valid = k_col + (b0 - i) * tq < q_row
        log_keep = jnp.where(valid, log_beta - z, 0.0)
        within = suffix_in_blocks(log_keep, SB_SPAN)
        r_run = jnp.zeros((rows, 1), F32)
        ws = [None] * SB_SPAN
        for b in reversed(range(SB_SPAN)):
            blk = slice(b * tq, (b + 1) * tq)
            w = jnp.exp(log_beta[:, blk] + within[b] + r_run)
            ws[b] = jnp.where(valid[:, blk], w, 0.0).astype(BF16)
            r_run = r_run + jnp.sum(log_keep[:, blk], axis=-1, keepdims=True)
        acc = jnp.dot(jnp.concatenate(ws, axis=1), v_ref[0, pl.ds(start, span), :], preferred_element_type=F32)
        return b0 - 1, r_run, acc

    def one_block(q2, j, r_run, acc):
        start = pl.multiple_of(j * tq, tq)
        z = _nt(q2, k_ref[0, pl.ds(start, tq), :])
        log_beta = _log_sigmoid(z)
        log_keep = log_beta - z
        w = jnp.exp(log_beta + suffix_in_blocks(log_keep, 1)[0] + r_run)
        acc = acc + jnp.dot(w.astype(BF16), v_ref[0, pl.ds(start, tq), :], preferred_element_type=F32)
        return r_run + jnp.sum(log_keep, axis=-1, keepdims=True), acc

    firsts = []
    for qb in range(nq):
        q = q_ref[0, qb * tq:(qb + 1) * tq, :]
        q2 = jnp.concatenate([jnp.where(in_head[h], q, jnp.zeros_like(q)) for h in range(n_heads)], axis=0)
        firsts.append((q2,) + first_span(q2, gi * nq + qb))

    for qb in range(nq):
        q2, j0, r0, acc0 = firsts[qb]

        def cond(carry):
            j, r_run, _ = carry
            return jnp.logical_and(j >= 0, jnp.max(r_run) > SB_EXP_ZERO)

        def body(carry, q2=q2):
            j, r_run, acc = carry
            r_run, acc = one_block(q2, j, r_run, acc)
            return j - 1, r_run, acc

        _, _, acc = lax.while_loop(cond, body, (j0, r0, acc0))

        o = jnp.where(in_head[0], acc[:tq], acc[tq:])
        o2 = o * o
        ms = jnp.zeros_like(o)
        for h in range(n_heads):
            ms_h = jnp.sum(jnp.where(in_head[h], o2, 0.0), axis=-1, keepdims=True) * (1.0 / SB_HEAD_DIM)
            ms = jnp.where(in_head[h], ms_h, ms)
        o_ref[0, qb * tq:(qb + 1) * tq, :] = (o * lax.rsqrt(ms + EPS) * g_ref[...]).astype(o_ref.dtype)


def _sb_attention(q, k, v, gain, tq=128, nq=2):
    b, s, w = q.shape
    pairs = w // LANES
    return pl.pallas_call(
        functools.partial(_sb_kernel, tq=tq, nq=nq),
        grid=(b, pairs, s // (tq * nq)),
        in_specs=[pl.BlockSpec((1, tq * nq, LANES), lambda bi, p, i: (bi, i, p)),
                  pl.BlockSpec((1, s, LANES), lambda bi, p, i: (bi, 0, p)),
                  pl.BlockSpec((1, s, LANES), lambda bi, p, i: (bi, 0, p)),
                  pl.BlockSpec((1, LANES), lambda bi, p, i: (0, p))],
        out_specs=pl.BlockSpec((1, tq * nq, LANES), lambda bi, p, i: (bi, i, p)),
        out_shape=jax.ShapeDtypeStruct((b, s, w), BF16),
        compiler_params=pltpu.CompilerParams(dimension_semantics=("arbitrary",) * 3, vmem_limit_bytes=VMEM_LIMIT),
        name="sb",
    )(q, k, v, gain)


def _gla_kernel(q_ref, k_ref, v_ref, la_ref, r_ref, g_ref, o_ref, st_ref):
    c = GLA_CHUNK
    nsub = c // GLA_SUB

    @pl.when(pl.program_id(1) == 0)
    def _():
        st_ref[...] = jnp.zeros_like(st_ref)

    la = la_ref[0]
    row = lax.broadcasted_iota(jnp.int32, (c, c), 0)
    col = lax.broadcasted_iota(jnp.int32, (c, c), 1)
    incl = jnp.where(col <= row, 1.0, 0.0).astype(F32)
    g_cum = jnp.dot(incl, la, precision=HIGHEST, preferred_element_type=F32)
    g_exc = g_cum - la
    g_last = g_cum[c - 1:c, :]
    refs = [g_exc[GLA_SUB * sb:GLA_SUB * sb + 1, :] for sb in range(nsub)]
    r_rows = jnp.concatenate([jnp.broadcast_to(r, (GLA_SUB, la.shape[1])) for r in refs], axis=0)
    e_q = jnp.exp(g_cum - r_rows)
    e_inter = jnp.exp(g_cum)
    e_k_last = jnp.exp(g_last - g_cum)
    e_last = jnp.exp(g_last)
    causal = col <= row
    rowk = lax.broadcasted_iota(jnp.int32, (c, GLA_DK), 0)

    q_all = q_ref[0].astype(F32)
    k_all = k_ref[0].astype(F32)
    v_all = v_ref[0]
    gate = r_ref[0].astype(F32)
    gate = gate * jax.nn.sigmoid(gate)
    for h in range(GLA_HEADS):
        ks = slice(h * GLA_DK, (h + 1) * GLA_DK)
        vs = slice(h * GLA_DV, (h + 1) * GLA_DV)
        q = q_all[:, ks]
        k = k_all[:, ks]
        v = v_all[:, vs]
        gh = g_cum[:, ks]
        qt = q * e_q[:, ks]
        q_hat = jnp.concatenate(
            [jnp.where((rowk >= GLA_SUB * sb) & (rowk < GLA_SUB * (sb + 1)), qt, 0.0) for sb in range(nsub)],
            axis=1).astype(BF16)
        k_hat = jnp.concatenate(
            [jnp.where(rowk < GLA_SUB * (sb + 1),
                       k * jnp.exp(jnp.minimum(refs[sb][:, ks] - gh, GLA_EXP_CLAMP)), 0.0) for sb in range(nsub)],
            axis=1).astype(BF16)
        scores = jnp.where(causal, _nt(q_hat, k_hat), 0.0)
        o_intra = jnp.dot(scores.astype(BF16), v, preferred_element_type=F32)
        st = st_ref[h]
        o_inter = _nt((q * e_inter[:, ks]).astype(BF16), st.astype(BF16))
        k_dec = (k * e_k_last[:, ks]).astype(BF16)
        st_ref[h] = st * e_last[:, ks] + _tn(v, k_dec)
        o = o_intra + o_inter
        o = o * _rms_scale(o) * g_ref[:, vs] * gate[:, vs]
        o_ref[0, :, vs] = o.astype(o_ref.dtype)


def _gla(q, k, v, la, r, gain):
    b, s, _ = q.shape
    c = GLA_CHUNK
    kw = GLA_HEADS * GLA_DK
    vw = GLA_HEADS * GLA_DV
    tok = lambda w: pl.BlockSpec((1, c, w), lambda bi, i: (bi, i, 0))
    return pl.pallas_call(
        _gla_kernel,
        grid=(b, s // c),
        in_specs=[tok(kw), tok(kw), tok(vw), tok(kw), tok(vw), pl.BlockSpec((1, vw), lambda bi, i: (0, 0))],
        out_specs=tok(vw),
        out_shape=jax.ShapeDtypeStruct((b, s, vw), BF16),
        scratch_shapes=[pltpu.VMEM((GLA_HEADS, GLA_DV, GLA_DK), F32)],
        compiler_params=pltpu.CompilerParams(dimension_semantics=("arbitrary", "arbitrary"), vmem_limit_bytes=VMEM_LIMIT),
        name="gla",
    )(q, k, v, la, r, gain)


def _mix_kernel(sb_ref, gla_ref, x_ref, gate_ref, shift_ref, scale_ref, wo_ref, wq_ref, ka_ref, kb_ref,
                x1_ref, h2_ref, st_ref):
    half = sb_ref.shape[2]
    mixed = (jnp.dot(sb_ref[0], wo_ref[:half, :], preferred_element_type=F32)
             + jnp.dot(gla_ref[0], wo_ref[half:, :], preferred_element_type=F32))
    x1 = x_ref[0] + gate_ref[0] * mixed
    x1_ref[0] = x1
    h2 = ((x1 * _rms_scale(x1)) * (1.0 + scale_ref[0]) + shift_ref[0]).astype(BF16)
    h2_ref[0] = h2
    for h in range(PEER_HEADS):
        base = 2 * PEER_HALF * h
        q1 = jnp.dot(h2, wq_ref[:, base:base + PEER_HALF], preferred_element_type=F32).astype(BF16)
        q2 = jnp.dot(h2, wq_ref[:, base + PEER_HALF:base + 2 * PEER_HALF], preferred_element_type=F32).astype(BF16)
        st_ref[2 * h] = _nt(ka_ref[h], q1)
        st_ref[2 * h + 1] = _nt(kb_ref[h], q2)


def _mix(sb, gla, x, gate, shift, scale, w_out, w_q, keys_a, keys_b, tm=256):
    b, s, d = x.shape
    ns = s // tm
    tok = lambda w: pl.BlockSpec((1, tm, w), lambda bi, i: (bi, i, 0))
    full = lambda a: pl.BlockSpec(a.shape, lambda bi, i: (0,) * a.ndim)
    mod = pl.BlockSpec((1, 1, d), lambda bi, i: (bi, 0, 0))
    return pl.pallas_call(
        _mix_kernel,
        grid=(b, ns),
        in_specs=[tok(sb.shape[2]), tok(gla.shape[2]), tok(d), mod, mod, mod,
                  full(w_out), full(w_q), full(keys_a), full(keys_b)],
        out_specs=[tok(d), tok(d),
                   pl.BlockSpec((2 * PEER_HEADS, PEER_KEYS, tm), lambda bi, i: (0, 0, bi * ns + i))],
        out_shape=[jax.ShapeDtypeStruct((b, s, d), F32), jax.ShapeDtypeStruct((b, s, d), BF16),
                   jax.ShapeDtypeStruct((2 * PEER_HEADS, PEER_KEYS, b * s), F32)],
        compiler_params=pltpu.CompilerParams(dimension_semantics=("arbitrary", "arbitrary"), vmem_limit_bytes=VMEM_LIMIT),
        name="mix",
    )(sb, gla, x, gate, shift, scale, w_out, w_q, keys_a, keys_b)


def _extract_max(s, iota, n):
    m = jnp.max(s, axis=0, keepdims=True)
    idx = jnp.min(jnp.where(s == m, iota, n), axis=0, keepdims=True)
    hit = iota == idx
    return m, idx, hit, jnp.where(hit, -jnp.inf, s)


_PAIR_COUNTS = tuple(PEER_TOPK // (a + 1) for a in range(PEER_TOPK))
_N_CAND = sum(_PAIR_COUNTS)
_N_CAND_PAD = -(-_N_CAND // 8) * 8


def _topk_kernel(s_ref, i_ref, j_ref, g_ref, cand_ref, ci_ref, cj_ref, oi_ref, oj_ref, og_ref):
    tl = s_ref.shape[2]
    k = PEER_TOPK
    iota_keys = lax.broadcasted_iota(jnp.int32, (PEER_KEYS, tl), 0)
    iota_cand = lax.broadcasted_iota(jnp.int32, (_N_CAND_PAD, tl), 0)

    def top_list(s):
        vals, idxs = [], []
        for _ in range(k):
            m, idx, _, s = _extract_max(s, iota_keys, PEER_KEYS)
            vals.append(m)
            idxs.append(idx)
        return vals, idxs

    def head(h, carry):
        v1, i1 = top_list(s_ref[2 * h])
        v2, i2 = top_list(s_ref[2 * h + 1])
        v2_all = jnp.concatenate(v2, axis=0)
        i2_all = jnp.concatenate(i2, axis=0)
        cand_ref[...] = jnp.full((_N_CAND_PAD, tl), -jnp.inf, F32)
        ci_ref[...] = jnp.zeros((_N_CAND_PAD, tl), jnp.int32)
        cj_ref[...] = jnp.zeros((_N_CAND_PAD, tl), jnp.int32)
        off = 0
        for a, nb in enumerate(_PAIR_COUNTS):
            cand_ref[off:off + nb, :] = v1[a] + v2_all[:nb]
            ci_ref[off:off + nb, :] = jnp.broadcast_to(i1[a], (nb, tl))
            cj_ref[off:off + nb, :] = i2_all[:nb]
            off += nb
        cand = cand_ref[...]
        ci = ci_ref[...]
        cj = cj_ref[...]
        tops, sel_i, sel_j = [], [], []
        for _ in range(k):
            m, _, hit, cand = _extract_max(cand, iota_cand, _N_CAND_PAD)
            tops.append(m)
            sel_i.append(jnp.sum(jnp.where(hit, ci, 0), axis=0, keepdims=True))
            sel_j.append(jnp.sum(jnp.where(hit, cj, 0), axis=0, keepdims=True))
        top = jnp.concatenate(tops, axis=0)
        e = jnp.exp(top - top[0:1])
        gates = e / jnp.sum(e, axis=0, keepdims=True)
        rows = pl.ds(pl.multiple_of(h * k, k), k)
        oi_ref[rows, :] = jnp.concatenate(sel_i, axis=0)
        oj_ref[rows, :] = jnp.concatenate(sel_j, axis=0)
        og_ref[rows, :] = gates
        return carry

    lax.fori_loop(0, PEER_HEADS, head, 0)
    for t0 in range(0, tl, LANES):
        i_ref[t0:t0 + LANES, :] = oi_ref[:, t0:t0 + LANES].T
        j_ref[t0:t0 + LANES, :] = oj_ref[:, t0:t0 + LANES].T
        g_ref[t0:t0 + LANES, :] = og_ref[:, t0:t0 + LANES].T


def _topk(scores_t, tl=128):
    nh2, nk, t = scores_t.shape
    nsel = PEER_HEADS * PEER_TOPK
    out = pl.BlockSpec((tl, nsel), lambda i: (i, 0))
    return pl.pallas_call(
        _topk_kernel,
        grid=(t // tl,),
        in_specs=[pl.BlockSpec((nh2, nk, tl), lambda i: (0, 0, i))],
        out_specs=[out, out, out],
        out_shape=[jax.ShapeDtypeStruct((t, nsel), jnp.int32), jax.ShapeDtypeStruct((t, nsel), jnp.int32),
                   jax.ShapeDtypeStruct((t, nsel), F32)],
        scratch_shapes=[pltpu.VMEM((_N_CAND_PAD, tl), F32), pltpu.VMEM((_N_CAND_PAD, tl), jnp.int32),
                        pltpu.VMEM((_N_CAND_PAD, tl), jnp.int32),
                        pltpu.VMEM((nsel, tl), jnp.int32), pltpu.VMEM((nsel, tl), jnp.int32), pltpu.VMEM((nsel, tl), F32)],
        compiler_params=pltpu.CompilerParams(dimension_semantics=("arbitrary",), vmem_limit_bytes=VMEM_LIMIT),
        name="topk",
    )(scores_t)


def _peer_kernel(h2_ref, u_ref, v_ref, isel_ref, jsel_ref, gsel_ref, x1_ref, gate_ref, gf_ref, o_ref,
                 gs_ref, acc_ref, *, tm, te):
    e = pl.program_id(1)
    n_i = te // PEER_KEYS
    half = PEER_KEYS // 2
    hi_mask = jnp.uint32(0xFFFF0000)

    @pl.when(e == 0)
    def _():
        acc_ref[...] = jnp.zeros_like(acc_ref)
        iota0 = lax.broadcasted_iota(jnp.int32, (PEER_KEYS, LANES), 0)

        def per_token(t, carry):
            isel = isel_ref[pl.ds(t, 1), :]
            jsel = jsel_ref[pl.ds(t, 1), :]
            g = gsel_ref[pl.ds(t, 1), :]
            gi = jnp.where(iota0 == isel, g, 0.0).astype(BF16)
            oj = jnp.where(iota0 == jsel, 1.0, 0.0).astype(BF16)
            gm = _nt(gi, oj).astype(BF16).astype(F32)
            bits = lax.bitcast_convert_type(gm, jnp.uint32)
            gs_ref[pl.ds(t * GS_PITCH, half), :] = (bits[:half] >> 16) | bits[half:]
            return carry

        lax.fori_loop(0, tm, per_token, 0, unroll=8)

    a = _nt(h2_ref[...], u_ref[...])
    act = 0.5 * a * (1.0 + lax.erf(a * (1.0 / math.sqrt(2.0))))
    i0 = e * n_i
    shift = jnp.where(i0 < half, 16, 0).astype(jnp.uint32)
    row0 = jnp.where(i0 < half, i0, i0 - half)
    parts = []
    for ii in range(n_i):
        word = gs_ref[pl.ds(row0 + ii, tm, stride=GS_PITCH), :]
        gm = lax.bitcast_convert_type((word << shift) & hi_mask, F32)
        parts.append((act[:, ii * PEER_KEYS:(ii + 1) * PEER_KEYS] * gm).astype(BF16))
    w = jnp.concatenate(parts, axis=1)
    acc_ref[...] += jnp.dot(w, v_ref[...], preferred_element_type=F32)

    @pl.when(e == pl.num_programs(1) - 1)
    def _():
        x2 = x1_ref[...] + gate_ref[0] * acc_ref[...]
        o_ref[...] = x2 * _rms_scale(x2) * gf_ref[...]


def _peer(h2, u, v, isel, jsel, gsel, x1, gate, g_final, seq, tm=512, te=512):
    t, d = h2.shape
    n_e = u.shape[0]
    per_batch = seq // tm
    assert (PEER_KEYS // 2) % (te // PEER_KEYS) == 0
    tok = lambda w: pl.BlockSpec((tm, w), lambda i, e: (i, 0))
    exp = pl.BlockSpec((te, d), lambda i, e: (e, 0))
    return pl.pallas_call(
        functools.partial(_peer_kernel, tm=tm, te=te),
        grid=(t // tm, n_e // te),
        in_specs=[tok(d), exp, exp, tok(isel.shape[1]), tok(isel.shape[1]), tok(isel.shape[1]), tok(d),
                  pl.BlockSpec((1, 1, d), lambda i, e: (i // per_batch, 0, 0)),
                  pl.BlockSpec((1, d), lambda i, e: (0, 0))],
        out_specs=tok(d),
        out_shape=jax.ShapeDtypeStruct((t, d), F32),
        scratch_shapes=[pltpu.VMEM((tm * GS_PITCH, LANES), jnp.uint32), pltpu.VMEM((tm, d), F32)],
        compiler_params=pltpu.CompilerParams(dimension_semantics=("arbitrary", "arbitrary"), vmem_limit_bytes=VMEM_LIMIT),
        name="peer",
    )(h2, u, v, isel, jsel, gsel, x1, gate, g_final)


def kernel(x, c, w_ada, b_ada, w_in, w_gla_gate_up, b_gla_gate, g_sb_norm, g_gla_norm, w_out, w_peer_q,
           peer_keys_a, peer_keys_b, peer_u, peer_v, g_final):
    b, s, d = x.shape
    depth = w_ada.shape[0]
    sbw = SB_HEADS * SB_HEAD_DIM
    gkw = GLA_HEADS * GLA_DK
    gvw = GLA_HEADS * GLA_DV
    lr0 = 3 * sbw + 2 * gkw + gvw
    c_pad = jnp.pad(c, ((0, 8 - b), (0, 0)))
    for l in range(depth):
        mod = _ada(c_pad, w_ada[l], b_ada[l][None, :])[:b]
        shift1, scale1, gate1, shift2, scale2, gate2 = (mod[:, None, i * d:(i + 1) * d] for i in range(6))

        w_l = w_in[l]
        w_main = jnp.concatenate([w_l[:, :lr0], w_l[:, lr0 + GLA_GATE_RANK:]], axis=1).astype(BF16)
        w_lr = jnp.pad(w_l[:, lr0:lr0 + GLA_GATE_RANK], ((0, 0), (0, LANES - GLA_GATE_RANK))).astype(BF16)
        w_up = jnp.pad(w_gla_gate_up[l], ((0, LANES - GLA_GATE_RANK), (0, 0)))
        sbq, sbk, sbv, gq, gk, gv, gr, la = _inproj(x, shift1, scale1, w_main, w_lr, w_up, b_gla_gate[l][None, :])

        sb_out = _sb_attention(sbq, sbk, sbv, g_sb_norm[l][None, :])
        gla_out = _gla(gq, gk, gv, la, gr, g_gla_norm[l][None, :])

        x1, h2, scores_t = _mix(sb_out, gla_out, x, gate1, shift2, scale2, w_out[l].astype(BF16),
                                w_peer_q[l].astype(BF16), peer_keys_a[l].astype(BF16), peer_keys_b[l].astype(BF16))
        isel, jsel, gsel = _topk(scores_t)
        x = _peer(h2.reshape(b * s, d), peer_u[l].astype(BF16), peer_v[l].astype(BF16), isel, jsel, gsel,
                  x1.reshape(b * s, d), gate2, g_final[None, :], s).reshape(b, s, d)
    return x
```

```python
import functools
import math

import jax
import jax.numpy as jnp
from jax import lax
from jax.experimental import pallas as pl
from jax.experimental.pallas import tpu as pltpu

F32 = jnp.float32
BF16 = jnp.bfloat16
HIGHEST = lax.Precision.HIGHEST

EPS = 1e-6
LANES = 128
SB_HEADS = 8
SB_HEAD_DIM = 64
GLA_HEADS = 4
GLA_DK = 64
GLA_DV = 128
GLA_GATE_RANK = 16
GLA_GATE_TAU = 16.0
GLA_CHUNK = 64
GLA_SUB = 16
GLA_EXP_CLAMP = 60.0
PEER_HEADS = 8
PEER_KEYS = 128
PEER_TOPK = 16
PEER_HALF = 128
SB_EXP_ZERO = -104.0
SB_SPAN = 3
GS_PITCH = PEER_KEYS + 4

VMEM_LIMIT = 56 * 1024 * 1024


def _nt(a, b):
    return lax.dot_general(a, b, (((1,), (1,)), ((), ())), preferred_element_type=F32)


def _tn(a, b):
    return lax.dot_general(a, b, (((0,), (0,)), ((), ())), preferred_element_type=F32)


def _log_sigmoid(z):
    return jnp.minimum(z, 0.0) - jnp.log1p(jnp.exp(-jnp.abs(z)))


def _rms_scale(x):
    return lax.rsqrt(jnp.mean(x * x, axis=-1, keepdims=True) + EPS)


def _ada_kernel(c_ref, w_ref, b_ref, o_ref):
    c = c_ref[...]
    c_act = c * jax.nn.sigmoid(c)
    o_ref[...] = jnp.dot(c_act, w_ref[...], precision=HIGHEST, preferred_element_type=F32) + b_ref[...]


def _ada(c_pad, w, b, tn=1536):
    rows, d = c_pad.shape
    n = w.shape[1]
    return pl.pallas_call(
        _ada_kernel,
        grid=(n // tn,),
        in_specs=[pl.BlockSpec((rows, d), lambda j: (0, 0)),
                  pl.BlockSpec((d, tn), lambda j: (0, j)),
                  pl.BlockSpec((1, tn), lambda j: (0, j))],
        out_specs=pl.BlockSpec((rows, tn), lambda j: (0, j)),
        out_shape=jax.ShapeDtypeStruct((rows, n), F32),
        compiler_params=pltpu.CompilerParams(dimension_semantics=("arbitrary",), vmem_limit_bytes=VMEM_LIMIT),
        name="ada",
    )(c_pad, w, b)


def _inproj_kernel(x_ref, shift_ref, scale_ref, wm_ref, wlr_ref, wup_ref, bg_ref,
                   sbq_ref, sbk_ref, sbv_ref, gq_ref, gk_ref, gv_ref, gr_ref, la_ref):
    x = x_ref[0]
    h = (x * _rms_scale(x)) * (1.0 + scale_ref[0]) + shift_ref[0]
    hb = h.astype(BF16)
    sbw = SB_HEADS * SB_HEAD_DIM
    gkw = GLA_HEADS * GLA_DK
    gvw = GLA_HEADS * GLA_DV
    outs = ((sbq_ref, sbw, 0.125), (sbk_ref, sbw, None), (sbv_ref, sbw, None),
            (gq_ref, gkw, 0.125), (gk_ref, gkw, None), (gv_ref, gvw, None), (gr_ref, gvw, None))
    off = 0
    for ref, width, mul in outs:
        p = jnp.dot(hb, wm_ref[:, off:off + width], preferred_element_type=F32)
        if mul is not None:
            p = p * mul
        ref[0] = p.astype(ref.dtype)
        off += width
    g_lr = jnp.dot(hb, wlr_ref[...], preferred_element_type=F32)
    u = jnp.dot(g_lr, wup_ref[...], precision=HIGHEST, preferred_element_type=F32) + bg_ref[...]
    la_ref[0] = _log_sigmoid(u) * (1.0 / GLA_GATE_TAU)


def _inproj(x, shift, scale, w_main, w_lr, w_up, b_gate, tm=512):
    b, s, d = x.shape
    sbw = SB_HEADS * SB_HEAD_DIM
    gkw = GLA_HEADS * GLA_DK
    gvw = GLA_HEADS * GLA_DV
    widths = (sbw, sbw, sbw, gkw, gkw, gvw, gvw)
    tok = lambda w: pl.BlockSpec((1, tm, w), lambda bi, i: (bi, i, 0))
    full = lambda a: pl.BlockSpec(a.shape, lambda bi, i: (0,) * a.ndim)
    mod = pl.BlockSpec((1, 1, d), lambda bi, i: (bi, 0, 0))
    return pl.pallas_call(
        _inproj_kernel,
        grid=(b, s // tm),
        in_specs=[tok(d), mod, mod, full(w_main), full(w_lr), full(w_up), full(b_gate)],
        out_specs=[tok(w) for w in widths] + [tok(gkw)],
        out_shape=[jax.ShapeDtypeStruct((b, s, w), BF16) for w in widths] + [jax.ShapeDtypeStruct((b, s, gkw), F32)],
        compiler_params=pltpu.CompilerParams(dimension_semantics=("arbitrary", "arbitrary"), vmem_limit_bytes=VMEM_LIMIT),
        name="inproj",
    )(x, shift, scale, w_main, w_lr, w_up, b_gate)


def _sb_kernel(q_ref, k_ref, v_ref, g_ref, o_ref, *, tq, nq):
    gi = pl.program_id(2)
    lane = lax.broadcasted_iota(jnp.int32, (tq, LANES), 1)
    row = lax.broadcasted_iota(jnp.int32, (tq, tq), 0)
    col = lax.broadcasted_iota(jnp.int32, (tq, tq), 1)
    later = jnp.where(row > col, 1.0, 0.0).astype(BF16)

    n_heads = LANES // SB_HEAD_DIM
    rows = n_heads * tq
    in_head = [(lane >= h * SB_HEAD_DIM) & (lane < (h + 1) * SB_HEAD_DIM) for h in range(n_heads)]
    span = SB_SPAN * tq
    q_row = lax.broadcasted_iota(jnp.int32, (rows, span), 0) & (tq - 1)
    k_col = lax.broadcasted_iota(jnp.int32, (rows, span), 1)

    def suffix_in_blocks(log_keep, nblk):
        parts = []
        for b in range(nblk):
            lk = log_keep[:, b * tq:(b + 1) * tq]
            hi = lk.astype(BF16)
            parts += [hi, (lk - hi.astype(F32)).astype(BF16)]
        cs = jnp.dot(jnp.concatenate(parts, axis=0), later, preferred_element_type=F32)
        return [cs[2 * b * rows:(2 * b + 1) * rows] + cs[(2 * b + 1) * rows:(2 * b + 2) * rows] for b in range(nblk)]

    def first_span(q2, i):
        b0 = jnp.maximum(i - (SB_SPAN - 1), 0)
        start = pl.multiple_of(b0 * tq, tq)
        z = _nt(q2, k_ref[0, pl.ds(start, span), :])
        log_beta = _log_sigmoid(z)
        valid = k_col + (b0 - i) * tq < q_row
        log_keep = jnp.where(valid, log_beta - z, 0.0)
        within = suffix_in_blocks(log_keep, SB_SPAN)
        r_run = jnp.zeros((rows, 1), F32)
        ws = [None] * SB_SPAN
        for b in reversed(range(SB_SPAN)):
            blk = slice(b * tq, (b + 1) * tq)
            w = jnp.exp(log_beta[:, blk] + within[b] + r_run)
            ws[b] = jnp.where(valid[:, blk], w, 0.0).astype(BF16)
            r_run = r_run + jnp.sum(log_keep[:, blk], axis=-1, keepdims=True)
        acc = jnp.dot(jnp.concatenate(ws, axis=1), v_ref[0, pl.ds(start, span), :], preferred_element_type=F32)
        return b0 - 1, r_run, acc

    def one_block(q2, j, r_run, acc):
        start = pl.multiple_of(j * tq, tq)
        z = _nt(q2, k_ref[0, pl.ds(start, tq), :])
        log_beta = _log_sigmoid(z)
        log_keep = log_beta - z
        w = jnp.exp(log_beta + suffix_in_blocks(log_keep, 1)[0] + r_run)
        acc = acc + jnp.dot(w.astype(BF16), v_ref[0, pl.ds(start, tq), :], preferred_element_type=F32)
        return r_run + jnp.sum(log_keep, axis=-1, keepdims=True), acc

    firsts = []
    for qb in range(nq):
        q = q_ref[0, qb * tq:(qb + 1) * tq, :]
        q2 = jnp.concatenate([jnp.where(in_head[h], q, jnp.zeros_like(q)) for h in range(n_heads)], axis=0)
        firsts.append((q2,) + first_span(q2, gi * nq + qb))

    for qb in range(nq):
        q2, j0, r0, acc0 = firsts[qb]

        def cond(carry):
            j, r_run, _ = carry
            return jnp.logical_and(j >= 0, jnp.max(r_run) > SB_EXP_ZERO)

        def body(carry, q2=q2):
            j, r_run, acc = carry
            r_run, acc = one_block(q2, j, r_run, acc)
            return j - 1, r_run, acc

        _, _, acc = lax.while_loop(cond, body, (j0, r0, acc0))

        o = jnp.where(in_head[0], acc[:tq], acc[tq:])
        o2 = o * o
        ms = jnp.zeros_like(o)
        for h in range(n_heads):
            ms_h = jnp.sum(jnp.where(in_head[h], o2, 0.0), axis=-1, keepdims=True) * (1.0 / SB_HEAD_DIM)
            ms = jnp.where(in_head[h], ms_h, ms)
        o_ref[0, qb * tq:(qb + 1) * tq, :] = (o * lax.rsqrt(ms + EPS) * g_ref[...]).astype(o_ref.dtype)


def _sb_attention(q, k, v, gain, tq=128, nq=2):
    b, s, w = q.shape
    pairs = w // LANES
    return pl.pallas_call(
        functools.partial(_sb_kernel, tq=tq, nq=nq),
        grid=(b, pairs, s // (tq * nq)),
        in_specs=[pl.BlockSpec((1, tq * nq, LANES), lambda bi, p, i: (bi, i, p)),
                  pl.BlockSpec((1, s, LANES), lambda bi, p, i: (bi, 0, p)),
                  pl.BlockSpec((1, s, LANES), lambda bi, p, i: (bi, 0, p)),
                  pl.BlockSpec((1, LANES), lambda bi, p, i: (0, p))],
        out_specs=pl.BlockSpec((1, tq * nq, LANES), lambda bi, p, i: (bi, i, p)),
        out_shape=jax.ShapeDtypeStruct((b, s, w), BF16),
        compiler_params=pltpu.CompilerParams(dimension_semantics=("arbitrary",) * 3, vmem_limit_bytes=VMEM_LIMIT),
        name="sb",
    )(q, k, v, gain)


def _gla_kernel(q_ref, k_ref, v_ref, la_ref, r_ref, g_ref, o_ref, st_ref):
    c = GLA_CHUNK
    nsub = c // GLA_SUB

    @pl.when(pl.program_id(1) == 0)
    def _():
        st_ref[...] = jnp.zeros_like(st_ref)

    la = la_ref[0]
    row = lax.broadcasted_iota(jnp.int32, (c, c), 0)
    col = lax.broadcasted_iota(jnp.int32, (c, c), 1)
    incl = jnp.where(col <= row, 1.0, 0.0).astype(F32)
    g_cum = jnp.dot(incl, la, precision=HIGHEST, preferred_element_type=F32)
    g_exc = g_cum - la
    g_last = g_cum[c - 1:c, :]
    refs = [g_exc[GLA_SUB * sb:GLA_SUB * sb + 1, :] for sb in range(nsub)]
    r_rows = jnp.concatenate([jnp.broadcast_to(r, (GLA_SUB, la.shape[1])) for r in refs], axis=0)
    e_q = jnp.exp(g_cum - r_rows)
    e_inter = jnp.exp(g_cum)
    e_k_last = jnp.exp(g_last - g_cum)
    e_last = jnp.exp(g_last)
    causal = col <= row
    rowk = lax.broadcasted_iota(jnp.int32, (c, GLA_DK), 0)

    q_all = q_ref[0].astype(F32)
    k_all = k_ref[0].astype(F32)
    v_all = v_ref[0]
    gate = r_ref[0].astype(F32)
    gate = gate * jax.nn.sigmoid(gate)
    for h in range(GLA_HEADS):
        ks = slice(h * GLA_DK, (h + 1) * GLA_DK)
        vs = slice(h * GLA_DV, (h + 1) * GLA_DV)
        q = q_all[:, ks]
        k = k_all[:, ks]
        v = v_all[:, vs]
        gh = g_cum[:, ks]
        qt = q * e_q[:, ks]
        q_hat = jnp.concatenate(
            [jnp.where((rowk >= GLA_SUB * sb) & (rowk < GLA_SUB * (sb + 1)), qt, 0.0) for sb in range(nsub)],
            axis=1).astype(BF16)
        k_hat = jnp.concatenate(
            [jnp.where(rowk < GLA_SUB * (sb + 1),
                       k * jnp.exp(jnp.minimum(refs[sb][:, ks] - gh, GLA_EXP_CLAMP)), 0.0) for sb in range(nsub)],
            axis=1).astype(BF16)
        scores = jnp.where(causal, _nt(q_hat, k_hat), 0.0)
        o_intra = jnp.dot(scores.astype(BF16), v, preferred_element_type=F32)
        st = st_ref[h]
        o_inter = _nt((q * e_inter[:, ks]).astype(BF16), st.astype(BF16))
        k_dec = (k * e_k_last[:, ks]).astype(BF16)
        st_ref[h] = st * e_last[:, ks] + _tn(v, k_dec)
        o = o_intra + o_inter
        o = o * _rms_scale(o) * g_ref[:, vs] * gate[:, vs]
        o_ref[0, :, vs] = o.astype(o_ref.dtype)


def _gla(q, k, v, la, r, gain):
    b, s, _ = q.shape
    c = GLA_CHUNK
    kw = GLA_HEADS * GLA_DK
    vw = GLA_HEADS * GLA_DV
    tok = lambda w: pl.BlockSpec((1, c, w), lambda bi, i: (bi, i, 0))
    return pl.pallas_call(
        _gla_kernel,
        grid=(b, s // c),
        in_specs=[tok(kw), tok(kw), tok(vw), tok(kw), tok(vw), pl.BlockSpec((1, vw), lambda bi, i: (0, 0))],
        out_specs=tok(vw),
        out_shape=jax.ShapeDtypeStruct((b, s, vw), BF16),
        scratch_shapes=[pltpu.VMEM((GLA_HEADS, GLA_DV, GLA_DK), F32)],
        compiler_params=pltpu.CompilerParams(dimension_semantics=("arbitrary", "arbitrary"), vmem_limit_bytes=VMEM_LIMIT),
        name="gla",
    )(q, k, v, la, r, gain)


def _mix_kernel(sb_ref, gla_ref, x_ref, gate_ref, shift_ref, scale_ref, wo_ref, wq_ref, ka_ref, kb_ref,
                x1_ref, h2_ref, st_ref):
    half = sb_ref.shape[2]
    mixed = (jnp.dot(sb_ref[0], wo_ref[:half, :], preferred_element_type=F32)
             + jnp.dot(gla_ref[0], wo_ref[half:, :], preferred_element_type=F32))
    x1 = x_ref[0] + gate_ref[0] * mixed
    x1_ref[0] = x1
    h2 = ((x1 * _rms_scale(x1)) * (1.0 + scale_ref[0]) + shift_ref[0]).astype(BF16)
    h2_ref[0] = h2
    q = jnp.dot(h2, wq_ref[...], preferred_element_type=F32).astype(BF16)
    for h in range(PEER_HEADS):
        base = 2 * PEER_HALF * h
        st_ref[2 * h] = _nt(ka_ref[h], q[:, base:base + PEER_HALF])
        st_ref[2 * h + 1] = _nt(kb_ref[h], q[:, base + PEER_HALF:base + 2 * PEER_HALF])


def _mix(sb, gla, x, gate, shift, scale, w_out, w_q, keys_a, keys_b, tm=256):
    b, s, d = x.shape
    ns = s // tm
    tok = lambda w: pl.BlockSpec((1, tm, w), lambda bi, i: (bi, i, 0))
    full = lambda a: pl.BlockSpec(a.shape, lambda bi, i: (0,) * a.ndim)
    mod = pl.BlockSpec((1, 1, d), lambda bi, i: (bi, 0, 0))
    return pl.pallas_call(
        _mix_kernel,
        grid=(b, ns),
        in_specs=[tok(sb.shape[2]), tok(gla.shape[2]), tok(d), mod, mod, mod,
                  full(w_out), full(w_q), full(keys_a), full(keys_b)],
        out_specs=[tok(d), tok(d),
                   pl.BlockSpec((2 * PEER_HEADS, PEER_KEYS, tm), lambda bi, i: (0, 0, bi * ns + i))],
        out_shape=[jax.ShapeDtypeStruct((b, s, d), F32), jax.ShapeDtypeStruct((b, s, d), BF16),
                   jax.ShapeDtypeStruct((2 * PEER_HEADS, PEER_KEYS, b * s), F32)],
        compiler_params=pltpu.CompilerParams(dimension_semantics=("arbitrary", "arbitrary"), vmem_limit_bytes=VMEM_LIMIT),
        name="mix",
    )(sb, gla, x, gate, shift, scale, w_out, w_q, keys_a, keys_b)


def _sort_network(n):
    pairs = []
    p = 1
    while p < n:
        k = p
        while k >= 1:
            for j in range(k % p, n - k, 2 * k):
                for i in range(min(k, n - j - k)):
                    if (i + j) // (2 * p) == (i + j + k) // (2 * p):
                        pairs.append((i + j, i + j + k))
            k //= 2
        p *= 2
    return pairs


SUBLANES = 8
_PAIR_COUNTS = tuple(PEER_TOPK // (a + 1) for a in range(PEER_TOPK))
_N_CAND = sum(_PAIR_COUNTS)
_N_CAND_PAD = -(-_N_CAND // SUBLANES) * SUBLANES


def _top_list(s, k):
    n = s.shape[0]
    depth = n // SUBLANES
    vals = [s[SUBLANES * v:SUBLANES * (v + 1), :] for v in range(depth)]
    base = lax.broadcasted_iota(jnp.int32, vals[0].shape, 0).astype(F32)
    idxs = [base + float(SUBLANES * v) for v in range(depth)]
    for lo, hi in _sort_network(depth):
        swap = (vals[hi] > vals[lo]) | ((vals[hi] == vals[lo]) & (idxs[hi] < idxs[lo]))
        vals[lo], vals[hi] = jnp.where(swap, vals[hi], vals[lo]), jnp.where(swap, vals[lo], vals[hi])
        idxs[lo], idxs[hi] = jnp.where(swap, idxs[hi], idxs[lo]), jnp.where(swap, idxs[lo], idxs[hi])
    top_v, top_i = [], []
    for step in range(k):
        m = jnp.max(vals[0], axis=0, keepdims=True)
        idx = jnp.min(jnp.where(vals[0] == m, idxs[0], float(n)), axis=0, keepdims=True)
        top_v.append(m)
        top_i.append(idx)
        hit = idxs[0] == idx
        live = min(depth, k - step)
        for p in range(live - 1):
            vals[p] = jnp.where(hit, vals[p + 1], vals[p])
            idxs[p] = jnp.where(hit, idxs[p + 1], idxs[p])
        if live == depth:
            vals[depth - 1] = jnp.where(hit, -jnp.inf, vals[depth - 1])
    return top_v, top_i


def _topk_kernel(s_ref, i_ref, j_ref, g_ref, cand_ref, code_ref, oc_ref, og_ref):
    tl = s_ref.shape[2]
    k = PEER_TOPK
    iota_cand = lax.broadcasted_iota(jnp.int32, (_N_CAND_PAD, tl), 0).astype(F32)

    def head(h, carry):
        top_v, top_i = _top_list(jnp.concatenate([s_ref[2 * h], s_ref[2 * h + 1]], axis=1), k)
        v1 = [v[:, :tl] for v in top_v]
        i1 = [i[:, :tl] for i in top_i]
        v2_all = jnp.concatenate([v[:, tl:] for v in top_v], axis=0)
        i2_all = jnp.concatenate([i[:, tl:] for i in top_i], axis=0)
        cand_ref[...] = jnp.full((_N_CAND_PAD, tl), -jnp.inf, F32)
        code_ref[...] = jnp.zeros((_N_CAND_PAD, tl), F32)
        off = 0
        for a, nb in enumerate(_PAIR_COUNTS):
            cand_ref[off:off + nb, :] = v1[a] + v2_all[:nb]
            code_ref[off:off + nb, :] = i1[a] * float(PEER_KEYS) + i2_all[:nb]
            off += nb
        cand = cand_ref[...]
        code = code_ref[...]
        tops, sel = [], []
        for _ in range(k):
            m = jnp.max(cand, axis=0, keepdims=True)
            pos = jnp.min(jnp.where(cand == m, iota_cand, float(_N_CAND_PAD)), axis=0, keepdims=True)
            hit = iota_cand == pos
            tops.append(m)
            sel.append(jnp.sum(jnp.where(hit, code, 0.0), axis=0, keepdims=True))
            cand = jnp.where(hit, -jnp.inf, cand)
        top = jnp.concatenate(tops, axis=0)
        e = jnp.exp(top - top[0:1])
        gates = e / jnp.sum(e, axis=0, keepdims=True)
        rows = pl.ds(pl.multiple_of(h * k, k), k)
        oc_ref[rows, :] = jnp.concatenate(sel, axis=0).astype(jnp.int32)
        og_ref[rows, :] = gates
        return carry

    lax.fori_loop(0, PEER_HEADS, head, 0)
    shift = PEER_KEYS.bit_length() - 1
    for t0 in range(0, tl, LANES):
        code = oc_ref[:, t0:t0 + LANES].T
        i_ref[t0:t0 + LANES, :] = code >> shift
        j_ref[t0:t0 + LANES, :] = code & (PEER_KEYS - 1)
        g_ref[t0:t0 + LANES, :] = og_ref[:, t0:t0 + LANES].T


def _topk(scores_t, tl=256):
    nh2, nk, t = scores_t.shape
    nsel = PEER_HEADS * PEER_TOPK
    out = pl.BlockSpec((tl, nsel), lambda i: (i, 0))
    return pl.pallas_call(
        _topk_kernel,
        grid=(t // tl,),
        in_specs=[pl.BlockSpec((nh2, nk, tl), lambda i: (0, 0, i))],
        out_specs=[out, out, out],
        out_shape=[jax.ShapeDtypeStruct((t, nsel), jnp.int32), jax.ShapeDtypeStruct((t, nsel), jnp.int32),
                   jax.ShapeDtypeStruct((t, nsel), F32)],
        scratch_shapes=[pltpu.VMEM((_N_CAND_PAD, tl), F32), pltpu.VMEM((_N_CAND_PAD, tl), F32),
                        pltpu.VMEM((nsel, tl), jnp.int32), pltpu.VMEM((nsel, tl), F32)],
        compiler_params=pltpu.CompilerParams(dimension_semantics=("arbitrary",), vmem_limit_bytes=VMEM_LIMIT),
        name="topk",
    )(scores_t)


def _peer_kernel(h2_ref, u_ref, v_ref, isel_ref, jsel_ref, gsel_ref, x1_ref, gate_ref, gf_ref, o_ref,
                 gs_ref, *, tm, te):
    e = pl.program_id(1)
    n_i = te // PEER_KEYS

    @pl.when(e == 0)
    def _():
        o_ref[...] = jnp.zeros_like(o_ref)
        iota0 = lax.broadcasted_iota(jnp.int32, (PEER_KEYS, LANES), 0)

        def per_token(t, carry):
            isel = isel_ref[pl.ds(t, 1), :]
            jsel = jsel_ref[pl.ds(t, 1), :]
            g = gsel_ref[pl.ds(t, 1), :]
            gi = jnp.where(iota0 == isel, g, 0.0).astype(BF16)
            oj = jnp.where(iota0 == jsel, 1.0, 0.0).astype(BF16)
            gs_ref[pl.ds(t * GS_PITCH, PEER_KEYS), :] = _nt(gi, oj)
            return carry

        lax.fori_loop(0, tm, per_token, 0, unroll=16)

    a = _nt(h2_ref[...], u_ref[...])
    act = 0.5 * a * (1.0 + lax.erf(a * (1.0 / math.sqrt(2.0))))
    parts = []
    for ii in range(n_i):
        gm = gs_ref[pl.ds(e * n_i + ii, tm, stride=GS_PITCH), :]
        parts.append((act[:, ii * PEER_KEYS:(ii + 1) * PEER_KEYS] * gm).astype(BF16))
    w = jnp.concatenate(parts, axis=1)
    o_ref[...] += jnp.dot(w, v_ref[...], preferred_element_type=F32)

    @pl.when(e == pl.num_programs(1) - 1)
    def _():
        x2 = x1_ref[...] + gate_ref[0] * o_ref[...]
        o_ref[...] = x2 * _rms_scale(x2) * gf_ref[...]


def _peer(h2, u, v, isel, jsel, gsel, x1, gate, g_final, seq, tm=512, te=512):
    t, d = h2.shape
    n_e = u.shape[0]
    per_batch = seq // tm
    tok = lambda w: pl.BlockSpec((tm, w), lambda i, e: (i, 0))
    exp = pl.BlockSpec((te, d), lambda i, e: (e, 0))
    return pl.pallas_call(
        functools.partial(_peer_kernel, tm=tm, te=te),
        grid=(t // tm, n_e // te),
        in_specs=[tok(d), exp, exp, tok(isel.shape[1]), tok(isel.shape[1]), tok(isel.shape[1]),
                  pl.BlockSpec((tm, d), lambda i, e: (i, 0), pipeline_mode=pl.Buffered(1)),
                  pl.BlockSpec((1, 1, d), lambda i, e: (i // per_batch, 0, 0)),
                  pl.BlockSpec((1, d), lambda i, e: (0, 0))],
        out_specs=tok(d),
        out_shape=jax.ShapeDtypeStruct((t, d), F32),
        scratch_shapes=[pltpu.VMEM((tm * GS_PITCH, LANES), F32)],
        compiler_params=pltpu.CompilerParams(dimension_semantics=("arbitrary", "arbitrary"), vmem_limit_bytes=VMEM_LIMIT),
        name="peer",
    )(h2, u, v, isel, jsel, gsel, x1, gate, g_final)


def kernel(x, c, w_ada, b_ada, w_in, w_gla_gate_up, b_gla_gate, g_sb_norm, g_gla_norm, w_out, w_peer_q,
           peer_keys_a, peer_keys_b, peer_u, peer_v, g_final):
    b, s, d = x.shape
    depth = w_ada.shape[0]
    sbw = SB_HEADS * SB_HEAD_DIM
    gkw = GLA_HEADS * GLA_DK
    gvw = GLA_HEADS * GLA_DV
    lr0 = 3 * sbw + 2 * gkw + gvw
    c_pad = jnp.pad(c, ((0, 8 - b), (0, 0)))
    for l in range(depth):
        mod = _ada(c_pad, w_ada[l], b_ada[l][None, :])[:b]
        shift1, scale1, gate1, shift2, scale2, gate2 = (mod[:, None, i * d:(i + 1) * d] for i in range(6))

        w_l = w_in[l]
        w_main = jnp.concatenate([w_l[:, :lr0], w_l[:, lr0 + GLA_GATE_RANK:]], axis=1).astype(BF16)
        w_lr = jnp.pad(w_l[:, lr0:lr0 + GLA_GATE_RANK], ((0, 0), (0, LANES - GLA_GATE_RANK))).astype(BF16)
        w_up = jnp.pad(w_gla_gate_up[l], ((0, LANES - GLA_GATE_RANK), (0, 0)))
        sbq, sbk, sbv, gq, gk, gv, gr, la = _inproj(x, shift1, scale1, w_main, w_lr, w_up, b_gla_gate[l][None, :])

        sb_out = _sb_attention(sbq, sbk, sbv, g_sb_norm[l][None, :])
        gla_out = _gla(gq, gk, gv, la, gr, g_gla_norm[l][None, :])

        x1, h2, scores_t = _mix(sb_out, gla_out, x, gate1, shift2, scale2, w_out[l].astype(BF16),
                                w_peer_q[l].astype(BF16), peer_keys_a[l].astype(BF16), peer_keys_b[l].astype(BF16))
        isel, jsel, gsel = _topk(scores_t)
        x = _peer(h2.reshape(b * s, d), peer_u[l].astype(BF16), peer_v[l].astype(BF16), isel, jsel, gsel,
                  x1.reshape(b * s, d), gate2, g_final[None, :], s).reshape(b, s, d)
    return x
```

```python
import functools
import math

import jax
import jax.numpy as jnp
from jax import lax
from jax.experimental import pallas as pl
from jax.experimental.pallas import tpu as pltpu

F32 = jnp.float32
BF16 = jnp.bfloat16
HIGHEST = lax.Precision.HIGHEST

EPS = 1e-6
LANES = 128
SB_HEADS = 8
SB_HEAD_DIM = 64
GLA_HEADS = 4
GLA_DK = 64
GLA_DV = 128
GLA_GATE_RANK = 16
GLA_GATE_TAU = 16.0
GLA_CHUNK = 64
GLA_SUB = 16
GLA_EXP_CLAMP = 60.0
PEER_HEADS = 8
PEER_KEYS = 128
PEER_TOPK = 16
PEER_HALF = 128
SB_EXP_ZERO = -104.0
SB_SPAN = 3
GS_PITCH = PEER_KEYS + 4

VMEM_LIMIT = 56 * 1024 * 1024


def _nt(a, b):
    return lax.dot_general(a, b, (((1,), (1,)), ((), ())), preferred_element_type=F32)


def _tn(a, b):
    return lax.dot_general(a, b, (((0,), (0,)), ((), ())), preferred_element_type=F32)


def _log_sigmoid(z):
    return jnp.minimum(z, 0.0) - jnp.log(1.0 + jnp.exp(-jnp.abs(z)))


def _rms_scale(x):
    return lax.rsqrt(jnp.mean(x * x, axis=-1, keepdims=True) + EPS)


def _ada_kernel(c_ref, w_ref, b_ref, o_ref):
    c = c_ref[...]
    c_act = c * jax.nn.sigmoid(c)
    o_ref[...] = jnp.dot(c_act, w_ref[...], precision=HIGHEST, preferred_element_type=F32) + b_ref[...]


def _ada(c_pad, w, b, tn=1536):
    rows, d = c_pad.shape
    n = w.shape[1]
    return pl.pallas_call(
        _ada_kernel,
        grid=(n // tn,),
        in_specs=[pl.BlockSpec((rows, d), lambda j: (0, 0)),
                  pl.BlockSpec((d, tn), lambda j: (0, j)),
                  pl.BlockSpec((1, tn), lambda j: (0, j))],
        out_specs=pl.BlockSpec((rows, tn), lambda j: (0, j)),
        out_shape=jax.ShapeDtypeStruct((rows, n), F32),
        compiler_params=pltpu.CompilerParams(dimension_semantics=("arbitrary",), vmem_limit_bytes=VMEM_LIMIT),
        name="ada",
    )(c_pad, w, b)


def _inproj_kernel(x_ref, shift_ref, scale_ref, wm_ref, wlr_ref, wup_ref, bg_ref,
                   sbq_ref, sbk_ref, sbv_ref, gq_ref, gk_ref, gv_ref, gr_ref, la_ref):
    x = x_ref[0]
    h = (x * _rms_scale(x)) * (1.0 + scale_ref[0]) + shift_ref[0]
    hb = h.astype(BF16)
    sbw = SB_HEADS * SB_HEAD_DIM
    gkw = GLA_HEADS * GLA_DK
    gvw = GLA_HEADS * GLA_DV
    outs = ((sbq_ref, sbw, 0.125), (sbk_ref, sbw, None), (sbv_ref, sbw, None),
            (gq_ref, gkw, 0.125), (gk_ref, gkw, None), (gv_ref, gvw, None), (gr_ref, gvw, None))
    off = 0
    for ref, width, mul in outs:
        p = jnp.dot(hb, wm_ref[:, off:off + width], preferred_element_type=F32)
        if mul is not None:
            p = p * mul
        ref[0] = p.astype(ref.dtype)
        off += width
    g_lr = jnp.dot(hb, wlr_ref[...], preferred_element_type=F32)
    u = jnp.dot(g_lr, wup_ref[...], precision=HIGHEST, preferred_element_type=F32) + bg_ref[...]
    la_ref[0] = _log_sigmoid(u) * (1.0 / GLA_GATE_TAU)


def _inproj(x, shift, scale, w_main, w_lr, w_up, b_gate, tm=512):
    b, s, d = x.shape
    sbw = SB_HEADS * SB_HEAD_DIM
    gkw = GLA_HEADS * GLA_DK
    gvw = GLA_HEADS * GLA_DV
    widths = (sbw, sbw, sbw, gkw, gkw, gvw, gvw)
    tok = lambda w: pl.BlockSpec((1, tm, w), lambda bi, i: (bi, i, 0))
    full = lambda a: pl.BlockSpec(a.shape, lambda bi, i: (0,) * a.ndim)
    mod = pl.BlockSpec((1, 1, d), lambda bi, i: (bi, 0, 0))
    return pl.pallas_call(
        _inproj_kernel,
        grid=(b, s // tm),
        in_specs=[tok(d), mod, mod, full(w_main), full(w_lr), full(w_up), full(b_gate)],
        out_specs=[tok(w) for w in widths] + [tok(gkw)],
        out_shape=[jax.ShapeDtypeStruct((b, s, w), BF16) for w in widths] + [jax.ShapeDtypeStruct((b, s, gkw), F32)],
        compiler_params=pltpu.CompilerParams(dimension_semantics=("arbitrary", "arbitrary"), vmem_limit_bytes=VMEM_LIMIT),
        name="inproj",
    )(x, shift, scale, w_main, w_lr, w_up, b_gate)


def _sb_kernel(q_ref, k_ref, v_ref, g_ref, o_ref, *, tq, nq):
    gi = pl.program_id(2)
    lane = lax.broadcasted_iota(jnp.int32, (tq, LANES), 1)
    row = lax.broadcasted_iota(jnp.int32, (tq, tq), 0)
    col = lax.broadcasted_iota(jnp.int32, (tq, tq), 1)
    later = jnp.where(row > col, 1.0, 0.0).astype(BF16)
    later2 = jnp.concatenate([later, later], axis=0)

    n_heads = LANES // SB_HEAD_DIM
    rows = n_heads * tq
    in_head = [(lane >= h * SB_HEAD_DIM) & (lane < (h + 1) * SB_HEAD_DIM) for h in range(n_heads)]
    span = SB_SPAN * tq
    q_row = lax.broadcasted_iota(jnp.int32, (rows, span), 0) & (tq - 1)
    k_col = lax.broadcasted_iota(jnp.int32, (rows, span), 1)

    def suffix_in_blocks(log_keep, nblk):
        parts = []
        for b in range(nblk):
            lk = log_keep[:, b * tq:(b + 1) * tq]
            hi = lk.astype(BF16)
            parts.append(jnp.concatenate([hi, (lk - hi.astype(F32)).astype(BF16)], axis=1))
        cs = jnp.dot(jnp.concatenate(parts, axis=0), later2, preferred_element_type=F32)
        return [cs[b * rows:(b + 1) * rows] for b in range(nblk)]

    def first_span(q2, i):
        b0 = jnp.maximum(i - (SB_SPAN - 1), 0)
        start = pl.multiple_of(b0 * tq, tq)
        z = _nt(q2, k_ref[0, pl.ds(start, span), :])
        log_beta = _log_sigmoid(z)
        valid = k_col + (b0 - i) * tq < q_row
        log_keep = jnp.where(valid, log_beta - z, 0.0)
        within = suffix_in_blocks(log_keep, SB_SPAN)
        r_run = jnp.zeros((rows, 1), F32)
        ws = [None] * SB_SPAN
        for b in reversed(range(SB_SPAN)):
            blk = slice(b * tq, (b + 1) * tq)
            w = jnp.exp(log_beta[:, blk] + within[b] + r_run)
            ws[b] = jnp.where(valid[:, blk], w, 0.0).astype(BF16)
            r_run = r_run + jnp.sum(log_keep[:, blk], axis=-1, keepdims=True)
        acc = jnp.dot(jnp.concatenate(ws, axis=1), v_ref[0, pl.ds(start, span), :], preferred_element_type=F32)
        return b0 - 1, r_run, acc

    def one_block(q2, j, r_run, acc):
        start = pl.multiple_of(j * tq, tq)
        z = _nt(q2, k_ref[0, pl.ds(start, tq), :])
        log_beta = _log_sigmoid(z)
        log_keep = log_beta - z
        w = jnp.exp(log_beta + suffix_in_blocks(log_keep, 1)[0] + r_run)
        acc = acc + jnp.dot(w.astype(BF16), v_ref[0, pl.ds(start, tq), :], preferred_element_type=F32)
        return r_run + jnp.sum(log_keep, axis=-1, keepdims=True), acc

    q2s = []
    for qb in range(nq):
        q = q_ref[0, qb * tq:(qb + 1) * tq, :]
        q2s.append(jnp.concatenate([jnp.where(in_head[h], q, jnp.zeros_like(q)) for h in range(n_heads)], axis=0))

    firsts = [first_span(q2s[qb], gi * nq + qb) for qb in range(nq)]

    for qb in range(nq):
        q2 = q2s[qb]
        j0, r0, acc0 = firsts[qb]

        def cond(carry):
            j, r_run, _ = carry
            return jnp.logical_and(j >= 0, jnp.max(r_run) > SB_EXP_ZERO)

        def body(carry, q2=q2):
            j, r_run, acc = carry
            r_run, acc = one_block(q2, j, r_run, acc)
            return j - 1, r_run, acc

        _, _, acc = lax.while_loop(cond, body, (j0, r0, acc0))

        o = jnp.where(in_head[0], acc[:tq], acc[tq:])
        o2 = o * o
        ms = jnp.zeros_like(o)
        for h in range(n_heads):
            ms_h = jnp.sum(jnp.where(in_head[h], o2, 0.0), axis=-1, keepdims=True) * (1.0 / SB_HEAD_DIM)
            ms = jnp.where(in_head[h], ms_h, ms)
        o_ref[0, qb * tq:(qb + 1) * tq, :] = (o * lax.rsqrt(ms + EPS) * g_ref[...]).astype(o_ref.dtype)


def _sb_attention(q, k, v, gain, tq=128, nq=2):
    b, s, w = q.shape
    pairs = w // LANES
    return pl.pallas_call(
        functools.partial(_sb_kernel, tq=tq, nq=nq),
        grid=(b, pairs, s // (tq * nq)),
        in_specs=[pl.BlockSpec((1, tq * nq, LANES), lambda bi, p, i: (bi, i, p)),
                  pl.BlockSpec((1, s, LANES), lambda bi, p, i: (bi, 0, p)),
                  pl.BlockSpec((1, s, LANES), lambda bi, p, i: (bi, 0, p)),
                  pl.BlockSpec((1, LANES), lambda bi, p, i: (0, p))],
        out_specs=pl.BlockSpec((1, tq * nq, LANES), lambda bi, p, i: (bi, i, p)),
        out_shape=jax.ShapeDtypeStruct((b, s, w), BF16),
        compiler_params=pltpu.CompilerParams(dimension_semantics=("arbitrary",) * 3, vmem_limit_bytes=VMEM_LIMIT),
        name="sb",
    )(q, k, v, gain)


def _gla_kernel(q_ref, k_ref, v_ref, la_ref, r_ref, g_ref, o_ref, st_ref):
    c = GLA_CHUNK
    nsub = c // GLA_SUB

    @pl.when(pl.program_id(1) == 0)
    def _():
        st_ref[...] = jnp.zeros_like(st_ref)

    la = la_ref[0]
    row = lax.broadcasted_iota(jnp.int32, (c, c), 0)
    col = lax.broadcasted_iota(jnp.int32, (c, c), 1)
    incl = jnp.where(col <= row, 1.0, 0.0).astype(F32)
    g_cum = jnp.dot(incl, la, precision=HIGHEST, preferred_element_type=F32)
    g_exc = g_cum - la
    g_last = g_cum[c - 1:c, :]
    refs = [g_exc[GLA_SUB * sb:GLA_SUB * sb + 1, :] for sb in range(nsub)]
    r_rows = jnp.concatenate([jnp.broadcast_to(r, (GLA_SUB, la.shape[1])) for r in refs], axis=0)
    e_q = jnp.exp(g_cum - r_rows)
    e_inter = jnp.exp(g_cum)
    e_k_last = jnp.exp(g_last - g_cum)
    e_last = jnp.exp(g_last)
    causal = col <= row
    rowk = lax.broadcasted_iota(jnp.int32, (c, GLA_DK), 0)

    q_all = q_ref[0].astype(F32)
    k_all = k_ref[0].astype(F32)
    v_all = v_ref[0]
    gate = r_ref[0].astype(F32)
    gate = gate * jax.nn.sigmoid(gate)
    for h in range(GLA_HEADS):
        ks = slice(h * GLA_DK, (h + 1) * GLA_DK)
        vs = slice(h * GLA_DV, (h + 1) * GLA_DV)
        q = q_all[:, ks]
        k = k_all[:, ks]
        v = v_all[:, vs]
        gh = g_cum[:, ks]
        qt = q * e_q[:, ks]
        q_hat = jnp.concatenate(
            [jnp.where((rowk >= GLA_SUB * sb) & (rowk < GLA_SUB * (sb + 1)), qt, 0.0) for sb in range(nsub)],
            axis=1).astype(BF16)
        k_hat = jnp.concatenate(
            [jnp.where(rowk < GLA_SUB * (sb + 1),
                       k * jnp.exp(jnp.minimum(refs[sb][:, ks] - gh, GLA_EXP_CLAMP)), 0.0) for sb in range(nsub)],
            axis=1).astype(BF16)
        scores = jnp.where(causal, _nt(q_hat, k_hat), 0.0)
        o_intra = jnp.dot(scores.astype(BF16), v, preferred_element_type=F32)
        st = st_ref[h]
        o_inter = _nt((q * e_inter[:, ks]).astype(BF16), st.astype(BF16))
        k_dec = (k * e_k_last[:, ks]).astype(BF16)
        st_ref[h] = st * e_last[:, ks] + _tn(v, k_dec)
        o = o_intra + o_inter
        o = o * _rms_scale(o) * g_ref[:, vs] * gate[:, vs]
        o_ref[0, :, vs] = o.astype(o_ref.dtype)


def _gla(q, k, v, la, r, gain):
    b, s, _ = q.shape
    c = GLA_CHUNK
    kw = GLA_HEADS * GLA_DK
    vw = GLA_HEADS * GLA_DV
    tok = lambda w: pl.BlockSpec((1, c, w), lambda bi, i: (bi, i, 0))
    return pl.pallas_call(
        _gla_kernel,
        grid=(b, s // c),
        in_specs=[tok(kw), tok(kw), tok(vw), tok(kw), tok(vw), pl.BlockSpec((1, vw), lambda bi, i: (0, 0))],
        out_specs=tok(vw),
        out_shape=jax.ShapeDtypeStruct((b, s, vw), BF16),
        scratch_shapes=[pltpu.VMEM((GLA_HEADS, GLA_DV, GLA_DK), F32)],
        compiler_params=pltpu.CompilerParams(dimension_semantics=("arbitrary", "arbitrary"), vmem_limit_bytes=VMEM_LIMIT),
        name="gla",
    )(q, k, v, la, r, gain)


def _mix_kernel(sb_ref, gla_ref, x_ref, gate_ref, shift_ref, scale_ref, wo_ref, wq_ref, ka_ref, kb_ref,
                x1_ref, h2_ref, st_ref):
    half = sb_ref.shape[2]
    mixed = (jnp.dot(sb_ref[0], wo_ref[:half, :], preferred_element_type=F32)
             + jnp.dot(gla_ref[0], wo_ref[half:, :], preferred_element_type=F32))
    x1 = x_ref[0] + gate_ref[0] * mixed
    x1_ref[0] = x1
    h2 = ((x1 * _rms_scale(x1)) * (1.0 + scale_ref[0]) + shift_ref[0]).astype(BF16)
    h2_ref[0] = h2
    q = jnp.dot(h2, wq_ref[...], preferred_element_type=F32).astype(BF16)
    for h in range(PEER_HEADS):
        base = 2 * PEER_HALF * h
        st_ref[2 * h] = _nt(ka_ref[h], q[:, base:base + PEER_HALF])
        st_ref[2 * h + 1] = _nt(kb_ref[h], q[:, base + PEER_HALF:base + 2 * PEER_HALF])


def _mix(sb, gla, x, gate, shift, scale, w_out, w_q, keys_a, keys_b, tm=256):
    b, s, d = x.shape
    ns = s // tm
    tok = lambda w: pl.BlockSpec((1, tm, w), lambda bi, i: (bi, i, 0))
    full = lambda a: pl.BlockSpec(a.shape, lambda bi, i: (0,) * a.ndim)
    mod = pl.BlockSpec((1, 1, d), lambda bi, i: (bi, 0, 0))
    return pl.pallas_call(
        _mix_kernel,
        grid=(b, ns),
        in_specs=[tok(sb.shape[2]), tok(gla.shape[2]), tok(d), mod, mod, mod,
                  full(w_out), full(w_q), full(keys_a), full(keys_b)],
        out_specs=[tok(d), tok(d),
                   pl.BlockSpec((2 * PEER_HEADS, PEER_KEYS, tm), lambda bi, i: (0, 0, bi * ns + i))],
        out_shape=[jax.ShapeDtypeStruct((b, s, d), F32), jax.ShapeDtypeStruct((b, s, d), BF16),
                   jax.ShapeDtypeStruct((2 * PEER_HEADS, PEER_KEYS, b * s), F32)],
        compiler_params=pltpu.CompilerParams(dimension_semantics=("arbitrary", "arbitrary"), vmem_limit_bytes=VMEM_LIMIT),
        name="mix",
    )(sb, gla, x, gate, shift, scale, w_out, w_q, keys_a, keys_b)


def _sort_network(n):
    pairs = []
    p = 1
    while p < n:
        k = p
        while k >= 1:
            for j in range(k % p, n - k, 2 * k):
                for i in range(min(k, n - j - k)):
                    if (i + j) // (2 * p) == (i + j + k) // (2 * p):
                        pairs.append((i + j, i + j + k))
            k //= 2
        p *= 2
    return pairs


SUBLANES = 8
_PAIR_COUNTS = tuple(PEER_TOPK // (a + 1) for a in range(PEER_TOPK))
_N_CAND = sum(_PAIR_COUNTS)
_N_CAND_PAD = -(-_N_CAND // SUBLANES) * SUBLANES


def _top_list(s, k):
    n = s.shape[0]
    depth = n // SUBLANES
    vals = [s[SUBLANES * v:SUBLANES * (v + 1), :] for v in range(depth)]
    base = lax.broadcasted_iota(jnp.int32, vals[0].shape, 0).astype(F32)
    idxs = [base + float(SUBLANES * v) for v in range(depth)]
    for lo, hi in _sort_network(depth):
        swap = (vals[hi] > vals[lo]) | ((vals[hi] == vals[lo]) & (idxs[hi] < idxs[lo]))
        vals[lo], vals[hi] = jnp.where(swap, vals[hi], vals[lo]), jnp.where(swap, vals[lo], vals[hi])
        idxs[lo], idxs[hi] = jnp.where(swap, idxs[hi], idxs[lo]), jnp.where(swap, idxs[lo], idxs[hi])
    top_v, top_i = [], []
    for step in range(k):
        m = jnp.max(vals[0], axis=0, keepdims=True)
        idx = jnp.min(jnp.where(vals[0] == m, idxs[0], float(n)), axis=0, keepdims=True)
        top_v.append(m)
        top_i.append(idx)
        hit = idxs[0] == idx
        live = min(depth, k - step)
        for p in range(live - 1):
            vals[p] = jnp.where(hit, vals[p + 1], vals[p])
            idxs[p] = jnp.where(hit, idxs[p + 1], idxs[p])
        if live == depth:
            vals[depth - 1] = jnp.where(hit, -jnp.inf, vals[depth - 1])
    return top_v, top_i


def _topk_kernel(s_ref, i_ref, j_ref, g_ref, cand_ref, code_ref, oc_ref, og_ref):
    tl = s_ref.shape[2]
    k = PEER_TOPK
    iota_cand = lax.broadcasted_iota(jnp.int32, (_N_CAND_PAD, tl), 0).astype(F32)

    def head(h, carry):
        top_v, top_i = _top_list(jnp.concatenate([s_ref[2 * h], s_ref[2 * h + 1]], axis=1), k)
        v1 = [v[:, :tl] for v in top_v]
        i1 = [i[:, :tl] for i in top_i]
        v2_all = jnp.concatenate([v[:, tl:] for v in top_v], axis=0)
        i2_all = jnp.concatenate([i[:, tl:] for i in top_i], axis=0)
        cand_ref[...] = jnp.full((_N_CAND_PAD, tl), -jnp.inf, F32)
        code_ref[...] = jnp.zeros((_N_CAND_PAD, tl), F32)
        off = 0
        for a, nb in enumerate(_PAIR_COUNTS):
            cand_ref[off:off + nb, :] = v1[a] + v2_all[:nb]
            code_ref[off:off + nb, :] = i1[a] * float(PEER_KEYS) + i2_all[:nb]
            off += nb
        cand = cand_ref[...]
        code = code_ref[...]
        tops, sel = [], []
        for _ in range(k):
            m = jnp.max(cand, axis=0, keepdims=True)
            pos = jnp.min(jnp.where(cand == m, iota_cand, float(_N_CAND_PAD)), axis=0, keepdims=True)
            hit = iota_cand == pos
            tops.append(m)
            sel.append(jnp.sum(jnp.where(hit, code, 0.0), axis=0, keepdims=True))
            cand = jnp.where(hit, -jnp.inf, cand)
        top = jnp.concatenate(tops, axis=0)
        e = jnp.exp(top - top[0:1])
        gates = e / jnp.sum(e, axis=0, keepdims=True)
        rows = pl.ds(pl.multiple_of(h * k, k), k)
        oc_ref[rows, :] = jnp.concatenate(sel, axis=0).astype(jnp.int32)
        og_ref[rows, :] = gates
        return carry

    lax.fori_loop(0, PEER_HEADS, head, 0)
    shift = PEER_KEYS.bit_length() - 1
    for t0 in range(0, tl, LANES):
        code = oc_ref[:, t0:t0 + LANES].T
        i_ref[t0:t0 + LANES, :] = code >> shift
        j_ref[t0:t0 + LANES, :] = code & (PEER_KEYS - 1)
        g_ref[t0:t0 + LANES, :] = og_ref[:, t0:t0 + LANES].T


def _topk(scores_t, tl=256):
    nh2, nk, t = scores_t.shape
    nsel = PEER_HEADS * PEER_TOPK
    out = pl.BlockSpec((tl, nsel), lambda i: (i, 0))
    return pl.pallas_call(
        _topk_kernel,
        grid=(t // tl,),
        in_specs=[pl.BlockSpec((nh2, nk, tl), lambda i: (0, 0, i))],
        out_specs=[out, out, out],
        out_shape=[jax.ShapeDtypeStruct((t, nsel), jnp.int32), jax.ShapeDtypeStruct((t, nsel), jnp.int32),
                   jax.ShapeDtypeStruct((t, nsel), F32)],
        scratch_shapes=[pltpu.VMEM((_N_CAND_PAD, tl), F32), pltpu.VMEM((_N_CAND_PAD, tl), F32),
                        pltpu.VMEM((nsel, tl), jnp.int32), pltpu.VMEM((nsel, tl), F32)],
        compiler_params=pltpu.CompilerParams(dimension_semantics=("arbitrary",), vmem_limit_bytes=VMEM_LIMIT),
        name="topk",
    )(scores_t)


def _peer_kernel(h2_ref, u_ref, v_ref, isel_ref, jsel_ref, gsel_ref, x1_ref, gate_ref, gf_ref, o_ref,
                 gs_ref, *, tm, te):
    e = pl.program_id(1)
    n_i = te // PEER_KEYS

    @pl.when(e == 0)
    def _():
        o_ref[...] = jnp.zeros_like(o_ref)
        iota0 = lax.broadcasted_iota(jnp.int32, (PEER_KEYS, LANES), 0)

        def per_token(t, carry):
            isel = isel_ref[pl.ds(t, 1), :]
            jsel = jsel_ref[pl.ds(t, 1), :]
            g = gsel_ref[pl.ds(t, 1), :]
            gi = jnp.where(iota0 == isel, g, 0.0).astype(BF16)
            oj = jnp.where(iota0 == jsel, 1.0, 0.0).astype(BF16)
            gs_ref[pl.ds(t * GS_PITCH, PEER_KEYS), :] = _nt(gi, oj)
            return carry

        lax.fori_loop(0, tm, per_token, 0, unroll=64)

    a = _nt(h2_ref[...], u_ref[...])
    act = 0.5 * a * (1.0 + lax.erf(a * (1.0 / math.sqrt(2.0))))
    parts = []
    for ii in range(n_i):
        gm = gs_ref[pl.ds(e * n_i + ii, tm, stride=GS_PITCH), :]
        parts.append((act[:, ii * PEER_KEYS:(ii + 1) * PEER_KEYS] * gm).astype(BF16))
    w = jnp.concatenate(parts, axis=1)
    o_ref[...] += jnp.dot(w, v_ref[...], preferred_element_type=F32)

    @pl.when(e == pl.num_programs(1) - 1)
    def _():
        x2 = x1_ref[...] + gate_ref[0] * o_ref[...]
        o_ref[...] = x2 * _rms_scale(x2) * gf_ref[...]


def _peer(h2, u, v, isel, jsel, gsel, x1, gate, g_final, seq, tm=512, te=1024):
    t, d = h2.shape
    n_e = u.shape[0]
    per_batch = seq // tm
    tok = lambda w: pl.BlockSpec((tm, w), lambda i, e: (i, 0))
    exp = pl.BlockSpec((te, d), lambda i, e: (e, 0))
    return pl.pallas_call(
        functools.partial(_peer_kernel, tm=tm, te=te),
        grid=(t // tm, n_e // te),
        in_specs=[tok(d), exp, exp, tok(isel.shape[1]), tok(isel.shape[1]), tok(isel.shape[1]),
                  pl.BlockSpec((tm, d), lambda i, e: (i, 0), pipeline_mode=pl.Buffered(1)),
                  pl.BlockSpec((1, 1, d), lambda i, e: (i // per_batch, 0, 0)),
                  pl.BlockSpec((1, d), lambda i, e: (0, 0))],
        out_specs=tok(d),
        out_shape=jax.ShapeDtypeStruct((t, d), F32),
        scratch_shapes=[pltpu.VMEM((tm * GS_PITCH, LANES), F32)],
        compiler_params=pltpu.CompilerParams(dimension_semantics=("arbitrary", "arbitrary"), vmem_limit_bytes=VMEM_LIMIT),
        name="peer",
    )(h2, u, v, isel, jsel, gsel, x1, gate, g_final)


def kernel(x, c, w_ada, b_ada, w_in, w_gla_gate_up, b_gla_gate, g_sb_norm, g_gla_norm, w_out, w_peer_q,
           peer_keys_a, peer_keys_b, peer_u, peer_v, g_final):
    b, s, d = x.shape
    depth = w_ada.shape[0]
    sbw = SB_HEADS * SB_HEAD_DIM
    gkw = GLA_HEADS * GLA_DK
    gvw = GLA_HEADS * GLA_DV
    lr0 = 3 * sbw + 2 * gkw + gvw
    c_pad = jnp.pad(c, ((0, 8 - b), (0, 0)))
    for l in range(depth):
        mod = _ada(c_pad, w_ada[l], b_ada[l][None, :])[:b]
        shift1, scale1, gate1, shift2, scale2, gate2 = (mod[:, None, i * d:(i + 1) * d] for i in range(6))

        w_l = w_in[l]
        w_main = jnp.concatenate([w_l[:, :lr0], w_l[:, lr0 + GLA_GATE_RANK:]], axis=1).astype(BF16)
        w_lr = jnp.pad(w_l[:, lr0:lr0 + GLA_GATE_RANK], ((0, 0), (0, LANES - GLA_GATE_RANK))).astype(BF16)
        w_up = jnp.pad(w_gla_gate_up[l], ((0, LANES - GLA_GATE_RANK), (0, 0)))
        sbq, sbk, sbv, gq, gk, gv, gr, la = _inproj(x, shift1, scale1, w_main, w_lr, w_up, b_gla_gate[l][None, :])

        sb_out = _sb_attention(sbq, sbk, sbv, g_sb_norm[l][None, :])
        gla_out = _gla(gq, gk, gv, la, gr, g_gla_norm[l][None, :])

        x1, h2, scores_t = _mix(sb_out, gla_out, x, gate1, shift2, scale2, w_out[l].astype(BF16),
                                w_peer_q[l].astype(BF16), peer_keys_a[l].astype(BF16), peer_keys_b[l].astype(BF16))
        isel, jsel, gsel = _topk(scores_t)
        x = _peer(h2.reshape(b * s, d), peer_u[l].astype(BF16), peer_v[l].astype(BF16), isel, jsel, gsel,
                  x1.reshape(b * s, d), gate2, g_final[None, :], s).reshape(b, s, d)
    return x
```

```python
import functools
import math

import jax
import jax.numpy as jnp
from jax import lax
from jax.experimental import pallas as pl
from jax.experimental.pallas import tpu as pltpu

F32 = jnp.float32
BF16 = jnp.bfloat16
HIGHEST = lax.Precision.HIGHEST

EPS = 1e-6
LANES = 128
SB_HEADS = 8
SB_HEAD_DIM = 64
GLA_HEADS = 4
GLA_DK = 64
GLA_DV = 128
GLA_GATE_RANK = 16
GLA_GATE_TAU = 16.0
GLA_CHUNK = 64
GLA_GROUP = 4
GLA_SUB = 16
GLA_EXP_CLAMP = 60.0
PEER_HEADS = 8
PEER_KEYS = 128
PEER_TOPK = 16
PEER_HALF = 128
SB_EXP_ZERO = -104.0
SB_SPAN = 3
GS_PITCH = PEER_KEYS + 4

VMEM_LIMIT = 56 * 1024 * 1024


def _nt(a, b):
    return lax.dot_general(a, b, (((1,), (1,)), ((), ())), preferred_element_type=F32)


def _tn(a, b):
    return lax.dot_general(a, b, (((0,), (0,)), ((), ())), preferred_element_type=F32)


def _log_sigmoid(z):
    return jnp.minimum(z, 0.0) - jnp.log(1.0 + jnp.exp(-jnp.abs(z)))


def _rms_scale(x):
    return lax.rsqrt(jnp.mean(x * x, axis=-1, keepdims=True) + EPS)


def _ada_kernel(c_ref, w_ref, b_ref, o_ref):
    c = c_ref[...]
    c_act = c * jax.nn.sigmoid(c)
    o_ref[...] = jnp.dot(c_act, w_ref[...], precision=HIGHEST, preferred_element_type=F32) + b_ref[...]


def _ada(c_pad, w, b, tn=1536):
    rows, d = c_pad.shape
    n = w.shape[1]
    return pl.pallas_call(
        _ada_kernel,
        grid=(n // tn,),
        in_specs=[pl.BlockSpec((rows, d), lambda j: (0, 0)),
                  pl.BlockSpec((d, tn), lambda j: (0, j)),
                  pl.BlockSpec((1, tn), lambda j: (0, j))],
        out_specs=pl.BlockSpec((rows, tn), lambda j: (0, j)),
        out_shape=jax.ShapeDtypeStruct((rows, n), F32),
        compiler_params=pltpu.CompilerParams(dimension_semantics=("arbitrary",), vmem_limit_bytes=VMEM_LIMIT),
        name="ada",
    )(c_pad, w, b)


def _inproj_kernel(x_ref, shift_ref, scale_ref, wm_ref, wlr_ref, wup_ref, bg_ref,
                   sbq_ref, sbk_ref, sbv_ref, gq_ref, gk_ref, gv_ref, gr_ref, la_ref):
    x = x_ref[0]
    h = (x * _rms_scale(x)) * (1.0 + scale_ref[0]) + shift_ref[0]
    hb = h.astype(BF16)
    sbw = SB_HEADS * SB_HEAD_DIM
    gkw = GLA_HEADS * GLA_DK
    gvw = GLA_HEADS * GLA_DV
    outs = ((sbq_ref, sbw, 0.125), (sbk_ref, sbw, None), (sbv_ref, sbw, None),
            (gq_ref, gkw, 0.125), (gk_ref, gkw, None), (gv_ref, gvw, None), (gr_ref, gvw, None))
    off = 0
    for ref, width, mul in outs:
        p = jnp.dot(hb, wm_ref[:, off:off + width], preferred_element_type=F32)
        if mul is not None:
            p = p * mul
        ref[0] = p.astype(ref.dtype)
        off += width
    g_lr = jnp.dot(hb, wlr_ref[...], preferred_element_type=F32)
    u = jnp.dot(g_lr, wup_ref[...], precision=HIGHEST, preferred_element_type=F32) + bg_ref[...]
    la_ref[0] = _log_sigmoid(u) * (1.0 / GLA_GATE_TAU)


def _inproj(x, shift, scale, w_main, w_lr, w_up, b_gate, tm=512):
    b, s, d = x.shape
    sbw = SB_HEADS * SB_HEAD_DIM
    gkw = GLA_HEADS * GLA_DK
    gvw = GLA_HEADS * GLA_DV
    widths = (sbw, sbw, sbw, gkw, gkw, gvw, gvw)
    tok = lambda w: pl.BlockSpec((1, tm, w), lambda bi, i: (bi, i, 0))
    full = lambda a: pl.BlockSpec(a.shape, lambda bi, i: (0,) * a.ndim)
    mod = pl.BlockSpec((1, 1, d), lambda bi, i: (bi, 0, 0))
    return pl.pallas_call(
        _inproj_kernel,
        grid=(b, s // tm),
        in_specs=[tok(d), mod, mod, full(w_main), full(w_lr), full(w_up), full(b_gate)],
        out_specs=[tok(w) for w in widths] + [tok(gkw)],
        out_shape=[jax.ShapeDtypeStruct((b, s, w), BF16) for w in widths] + [jax.ShapeDtypeStruct((b, s, gkw), F32)],
        compiler_params=pltpu.CompilerParams(dimension_semantics=("arbitrary", "arbitrary"), vmem_limit_bytes=VMEM_LIMIT),
        name="inproj",
    )(x, shift, scale, w_main, w_lr, w_up, b_gate)


def _sb_kernel(q_ref, k_ref, v_ref, g_ref, o_ref, *, tq, nq):
    gi = pl.program_id(2)
    lane = lax.broadcasted_iota(jnp.int32, (tq, LANES), 1)
    row = lax.broadcasted_iota(jnp.int32, (tq, tq), 0)
    col = lax.broadcasted_iota(jnp.int32, (tq, tq), 1)
    later = jnp.where(row > col, 1.0, 0.0).astype(BF16)
    later2 = jnp.concatenate([later, later], axis=0)

    n_heads = LANES // SB_HEAD_DIM
    tqs = nq * tq
    rows = n_heads * tqs
    in_head = [(lane >= h * SB_HEAD_DIM) & (lane < (h + 1) * SB_HEAD_DIM) for h in range(n_heads)]
    nspan = SB_SPAN - 1 + nq
    span = nspan * tq
    q_row = lax.broadcasted_iota(jnp.int32, (rows, span), 0) & (tqs - 1)
    k_col = lax.broadcasted_iota(jnp.int32, (rows, span), 1)

    def suffix_in_blocks(log_keep, nblk):
        parts = []
        for b in range(nblk):
            lk = log_keep[:, b * tq:(b + 1) * tq]
            hi = lk.astype(BF16)
            parts.append(jnp.concatenate([hi, (lk - hi.astype(F32)).astype(BF16)], axis=1))
        cs = jnp.dot(jnp.concatenate(parts, axis=0), later2, preferred_element_type=F32)
        return [cs[b * rows:(b + 1) * rows] for b in range(nblk)]

    def first_span(q2, i):
        b0 = jnp.maximum(i - (SB_SPAN - 1), 0)
        start = pl.multiple_of(b0 * tq, tq)
        z = _nt(q2, k_ref[0, pl.ds(start, span), :])
        log_beta = _log_sigmoid(z)
        valid = k_col + (b0 - i) * tq < q_row
        log_keep = jnp.where(valid, log_beta - z, 0.0)
        within = suffix_in_blocks(log_keep, nspan)
        r_run = jnp.zeros((rows, 1), F32)
        ws = [None] * nspan
        for b in reversed(range(nspan)):
            blk = slice(b * tq, (b + 1) * tq)
            w = jnp.exp(log_beta[:, blk] + within[b] + r_run)
            ws[b] = jnp.where(valid[:, blk], w, 0.0).astype(BF16)
            r_run = r_run + jnp.sum(log_keep[:, blk], axis=-1, keepdims=True)
        acc = jnp.dot(jnp.concatenate(ws, axis=1), v_ref[0, pl.ds(start, span), :], preferred_element_type=F32)
        return b0 - 1, r_run, acc

    def one_block(q2, j, r_run, acc):
        start = pl.multiple_of(j * tq, tq)
        z = _nt(q2, k_ref[0, pl.ds(start, tq), :])
        log_beta = _log_sigmoid(z)
        log_keep = log_beta - z
        w = jnp.exp(log_beta + suffix_in_blocks(log_keep, 1)[0] + r_run)
        acc = acc + jnp.dot(w.astype(BF16), v_ref[0, pl.ds(start, tq), :], preferred_element_type=F32)
        return r_run + jnp.sum(log_keep, axis=-1, keepdims=True), acc

    q = q_ref[0]
    lane_q = lax.broadcasted_iota(jnp.int32, q.shape, 1)
    q2 = jnp.concatenate(
        [jnp.where((lane_q >= h * SB_HEAD_DIM) & (lane_q < (h + 1) * SB_HEAD_DIM), q, jnp.zeros_like(q))
         for h in range(n_heads)], axis=0)
    j0, r0, acc0 = first_span(q2, gi * nq)

    def cond(carry):
        j, r_run, _ = carry
        return jnp.logical_and(j >= 0, jnp.max(r_run) > SB_EXP_ZERO)

    def body(carry):
        j, r_run, acc = carry
        r_run, acc = one_block(q2, j, r_run, acc)
        return j - 1, r_run, acc

    _, _, acc = lax.while_loop(cond, body, (j0, r0, acc0))

    for qb in range(nq):
        rows_qb = slice(qb * tq, (qb + 1) * tq)
        o = jnp.where(in_head[0], acc[:tqs][rows_qb], acc[tqs:][rows_qb])
        o2 = o * o
        ms = jnp.zeros_like(o)
        for h in range(n_heads):
            ms_h = jnp.sum(jnp.where(in_head[h], o2, 0.0), axis=-1, keepdims=True) * (1.0 / SB_HEAD_DIM)
            ms = jnp.where(in_head[h], ms_h, ms)
        o_ref[0, rows_qb, :] = (o * lax.rsqrt(ms + EPS) * g_ref[...]).astype(o_ref.dtype)


def _sb_attention(q, k, v, gain, tq=128, nq=2):
    b, s, w = q.shape
    pairs = w // LANES
    return pl.pallas_call(
        functools.partial(_sb_kernel, tq=tq, nq=nq),
        grid=(b, pairs, s // (tq * nq)),
        in_specs=[pl.BlockSpec((1, tq * nq, LANES), lambda bi, p, i: (bi, i, p)),
                  pl.BlockSpec((1, s, LANES), lambda bi, p, i: (bi, 0, p)),
                  pl.BlockSpec((1, s, LANES), lambda bi, p, i: (bi, 0, p)),
                  pl.BlockSpec((1, LANES), lambda bi, p, i: (0, p))],
        out_specs=pl.BlockSpec((1, tq * nq, LANES), lambda bi, p, i: (bi, i, p)),
        out_shape=jax.ShapeDtypeStruct((b, s, w), BF16),
        compiler_params=pltpu.CompilerParams(dimension_semantics=("arbitrary",) * 3, vmem_limit_bytes=VMEM_LIMIT),
        name="sb",
    )(q, k, v, gain)


def _gla_kernel(q_ref, k_ref, v_ref, la_ref, r_ref, g_ref, o_ref, st_ref):
    c = GLA_CHUNK
    nsub = c // GLA_SUB
    n = GLA_GROUP * c
    w = la_ref.shape[2]

    @pl.when(pl.program_id(1) == 0)
    def _():
        st_ref[...] = jnp.zeros_like(st_ref)

    la = la_ref[0]
    row = lax.broadcasted_iota(jnp.int32, (n, n), 0)
    col = lax.broadcasted_iota(jnp.int32, (n, n), 1)
    incl = jnp.where(col <= row, 1.0, 0.0).astype(F32)
    g_cum = jnp.dot(incl, la, precision=HIGHEST, preferred_element_type=F32)
    g_exc = g_cum - la
    g_last = g_cum[n - 1:n, :]

    def rows_from(starts, count):
        return jnp.concatenate([jnp.broadcast_to(g_exc[s0:s0 + 1, :], (count, w)) for s0 in starts], axis=0)

    sub_ref = rows_from(range(0, n, GLA_SUB), GLA_SUB)
    chunk_ref = rows_from(range(0, n, c), c)
    e_q_sub = jnp.exp(g_cum - sub_ref)
    e_q_chunk = jnp.exp(g_cum - chunk_ref)
    e_inter = jnp.exp(g_cum)
    e_k_last = jnp.exp(g_last - g_cum)
    e_last = jnp.exp(g_last)
    e_k_sub = [jnp.exp(jnp.minimum(rows_from([c * ch + GLA_SUB * sb for ch in range(GLA_GROUP)], c) - g_cum,
                                   GLA_EXP_CLAMP)) for sb in range(nsub)]
    e_k_chunk = [jnp.exp(jnp.minimum(g_exc[c * ch:c * ch + 1, :] - g_cum, 0.0)) for ch in range(1, GLA_GROUP)]

    rowk = lax.broadcasted_iota(jnp.int32, (n, GLA_DK), 0)
    c_shift = c.bit_length() - 1
    sub_of_row = (rowk & (c - 1)) >> (GLA_SUB.bit_length() - 1)
    chunk_of_row = rowk >> c_shift
    intra = ((row >> c_shift) == (col >> c_shift)) & (col <= row)
    earlier = (col >> c_shift) < (row >> c_shift)

    q_all = q_ref[0].astype(F32)
    k_all = k_ref[0].astype(F32)
    v_all = v_ref[0]
    gate = r_ref[0].astype(F32)
    gate = gate * jax.nn.sigmoid(gate)
    for h in range(GLA_HEADS):
        ks = slice(h * GLA_DK, (h + 1) * GLA_DK)
        vs = slice(h * GLA_DV, (h + 1) * GLA_DV)
        q = q_all[:, ks]
        k = k_all[:, ks]
        v = v_all[:, vs]
        qt = q * e_q_sub[:, ks]
        q_hat = jnp.concatenate([jnp.where(sub_of_row == sb, qt, 0.0) for sb in range(nsub)], axis=1).astype(BF16)
        k_hat = jnp.concatenate(
            [jnp.where((rowk & (c - 1)) < GLA_SUB * (sb + 1), k * e_k_sub[sb][:, ks], 0.0) for sb in range(nsub)],
            axis=1).astype(BF16)
        scores = jnp.where(intra, _nt(q_hat, k_hat), 0.0)
        if GLA_GROUP > 1:
            qc = q * e_q_chunk[:, ks]
            qc_hat = jnp.concatenate([jnp.where(chunk_of_row == ch, qc, 0.0) for ch in range(1, GLA_GROUP)],
                                     axis=1).astype(BF16)
            kc_hat = jnp.concatenate(
                [jnp.where(chunk_of_row < ch, k * e_k_chunk[ch - 1][:, ks], 0.0) for ch in range(1, GLA_GROUP)],
                axis=1).astype(BF16)
            scores = jnp.where(earlier, _nt(qc_hat, kc_hat), scores)
        o_intra = jnp.dot(scores.astype(BF16), v, preferred_element_type=F32)
        st = st_ref[h]
        o_inter = _nt((q * e_inter[:, ks]).astype(BF16), st.astype(BF16))
        k_dec = (k * e_k_last[:, ks]).astype(BF16)
        st_ref[h] = st * e_last[:, ks] + _tn(v, k_dec)
        o = o_intra + o_inter
        o = o * _rms_scale(o) * g_ref[:, vs] * gate[:, vs]
        o_ref[0, :, vs] = o.astype(o_ref.dtype)


def _gla(q, k, v, la, r, gain):
    b, s, _ = q.shape
    c = GLA_GROUP * GLA_CHUNK
    kw = GLA_HEADS * GLA_DK
    vw = GLA_HEADS * GLA_DV
    tok = lambda w: pl.BlockSpec((1, c, w), lambda bi, i: (bi, i, 0))
    return pl.pallas_call(
        _gla_kernel,
        grid=(b, s // c),
        in_specs=[tok(kw), tok(kw), tok(vw), tok(kw), tok(vw), pl.BlockSpec((1, vw), lambda bi, i: (0, 0))],
        out_specs=tok(vw),
        out_shape=jax.ShapeDtypeStruct((b, s, vw), BF16),
        scratch_shapes=[pltpu.VMEM((GLA_HEADS, GLA_DV, GLA_DK), F32)],
        compiler_params=pltpu.CompilerParams(dimension_semantics=("arbitrary", "arbitrary"), vmem_limit_bytes=VMEM_LIMIT),
        name="gla",
    )(q, k, v, la, r, gain)


def _mix_kernel(sb_ref, gla_ref, x_ref, gate_ref, shift_ref, scale_ref, wo_ref, wq_ref, ka_ref, kb_ref,
                x1_ref, h2_ref, st_ref):
    half = sb_ref.shape[2]
    mixed = (jnp.dot(sb_ref[0], wo_ref[:half, :], preferred_element_type=F32)
             + jnp.dot(gla_ref[0], wo_ref[half:, :], preferred_element_type=F32))
    x1 = x_ref[0] + gate_ref[0] * mixed
    x1_ref[0] = x1
    h2 = ((x1 * _rms_scale(x1)) * (1.0 + scale_ref[0]) + shift_ref[0]).astype(BF16)
    h2_ref[0] = h2
    q = jnp.dot(h2, wq_ref[...], preferred_element_type=F32).astype(BF16)
    for h in range(PEER_HEADS):
        base = 2 * PEER_HALF * h
        st_ref[2 * h] = _nt(ka_ref[h], q[:, base:base + PEER_HALF])
        st_ref[2 * h + 1] = _nt(kb_ref[h], q[:, base + PEER_HALF:base + 2 * PEER_HALF])


def _mix(sb, gla, x, gate, shift, scale, w_out, w_q, keys_a, keys_b, tm=256):
    b, s, d = x.shape
    ns = s // tm
    tok = lambda w: pl.BlockSpec((1, tm, w), lambda bi, i: (bi, i, 0))
    full = lambda a: pl.BlockSpec(a.shape, lambda bi, i: (0,) * a.ndim)
    mod = pl.BlockSpec((1, 1, d), lambda bi, i: (bi, 0, 0))
    return pl.pallas_call(
        _mix_kernel,
        grid=(b, ns),
        in_specs=[tok(sb.shape[2]), tok(gla.shape[2]), tok(d), mod, mod, mod,
                  full(w_out), full(w_q), full(keys_a), full(keys_b)],
        out_specs=[tok(d), tok(d),
                   pl.BlockSpec((2 * PEER_HEADS, PEER_KEYS, tm), lambda bi, i: (0, 0, bi * ns + i))],
        out_shape=[jax.ShapeDtypeStruct((b, s, d), F32), jax.ShapeDtypeStruct((b, s, d), BF16),
                   jax.ShapeDtypeStruct((2 * PEER_HEADS, PEER_KEYS, b * s), F32)],
        compiler_params=pltpu.CompilerParams(dimension_semantics=("arbitrary", "arbitrary"), vmem_limit_bytes=VMEM_LIMIT),
        name="mix",
    )(sb, gla, x, gate, shift, scale, w_out, w_q, keys_a, keys_b)


def _sort_network(n):
    pairs = []
    p = 1
    while p < n:
        k = p
        while k >= 1:
            for j in range(k % p, n - k, 2 * k):
                for i in range(min(k, n - j - k)):
                    if (i + j) // (2 * p) == (i + j + k) // (2 * p):
                        pairs.append((i + j, i + j + k))
            k //= 2
        p *= 2
    return pairs


SUBLANES = 8
_PAIR_COUNTS = tuple(PEER_TOPK // (a + 1) for a in range(PEER_TOPK))
_N_CAND = sum(_PAIR_COUNTS)
_N_CAND_PAD = -(-_N_CAND // SUBLANES) * SUBLANES


def _top_list(s, k):
    n = s.shape[0]
    depth = n // SUBLANES
    vals = [s[SUBLANES * v:SUBLANES * (v + 1), :] for v in range(depth)]
    base = lax.broadcasted_iota(jnp.int32, vals[0].shape, 0).astype(F32)
    idxs = [base + float(SUBLANES * v) for v in range(depth)]
    for lo, hi in _sort_network(depth):
        swap = (vals[hi] > vals[lo]) | ((vals[hi] == vals[lo]) & (idxs[hi] < idxs[lo]))
        vals[lo], vals[hi] = jnp.where(swap, vals[hi], vals[lo]), jnp.where(swap, vals[lo], vals[hi])
        idxs[lo], idxs[hi] = jnp.where(swap, idxs[hi], idxs[lo]), jnp.where(swap, idxs[lo], idxs[hi])
    top_v, top_i = [], []
    for step in range(k):
        m = jnp.max(vals[0], axis=0, keepdims=True)
        idx = jnp.min(jnp.where(vals[0] == m, idxs[0], float(n)), axis=0, keepdims=True)
        top_v.append(m)
        top_i.append(idx)
        hit = idxs[0] == idx
        live = min(depth, k - step)
        for p in range(live - 1):
            vals[p] = jnp.where(hit, vals[p + 1], vals[p])
            idxs[p] = jnp.where(hit, idxs[p + 1], idxs[p])
        if live == depth:
            vals[depth - 1] = jnp.where(hit, -jnp.inf, vals[depth - 1])
    return top_v, top_i


def _topk_kernel(s_ref, i_ref, j_ref, g_ref, cand_ref, code_ref, oc_ref, og_ref):
    tl = s_ref.shape[2]
    k = PEER_TOPK
    iota_cand = lax.broadcasted_iota(jnp.int32, (_N_CAND_PAD, tl), 0).astype(F32)

    def head(h, carry):
        top_v, top_i = _top_list(jnp.concatenate([s_ref[2 * h], s_ref[2 * h + 1]], axis=1), k)
        v1 = [v[:, :tl] for v in top_v]
        i1 = [i[:, :tl] for i in top_i]
        v2_all = jnp.concatenate([v[:, tl:] for v in top_v], axis=0)
        i2_all = jnp.concatenate([i[:, tl:] for i in top_i], axis=0)
        cand_ref[...] = jnp.full((_N_CAND_PAD, tl), -jnp.inf, F32)
        code_ref[...] = jnp.zeros((_N_CAND_PAD, tl), F32)
        off = 0
        for a, nb in enumerate(_PAIR_COUNTS):
            cand_ref[off:off + nb, :] = v1[a] + v2_all[:nb]
            code_ref[off:off + nb, :] = i1[a] * float(PEER_KEYS) + i2_all[:nb]
            off += nb
        cand = cand_ref[...]
        code = code_ref[...]
        tops, sel = [], []
        for _ in range(k):
            m = jnp.max(cand, axis=0, keepdims=True)
            pos = jnp.min(jnp.where(cand == m, iota_cand, float(_N_CAND_PAD)), axis=0, keepdims=True)
            hit = iota_cand == pos
            tops.append(m)
            sel.append(jnp.sum(jnp.where(hit, code, 0.0), axis=0, keepdims=True))
            cand = jnp.where(hit, -jnp.inf, cand)
        top = jnp.concatenate(tops, axis=0)
        e = jnp.exp(top - top[0:1])
        gates = e / jnp.sum(e, axis=0, keepdims=True)
        rows = pl.ds(pl.multiple_of(h * k, k), k)
        oc_ref[rows, :] = jnp.concatenate(sel, axis=0).astype(jnp.int32)
        og_ref[rows, :] = gates
        return carry

    lax.fori_loop(0, PEER_HEADS, head, 0)
    shift = PEER_KEYS.bit_length() - 1
    for t0 in range(0, tl, LANES):
        code = oc_ref[:, t0:t0 + LANES].T
        i_ref[t0:t0 + LANES, :] = code >> shift
        j_ref[t0:t0 + LANES, :] = code & (PEER_KEYS - 1)
        g_ref[t0:t0 + LANES, :] = og_ref[:, t0:t0 + LANES].T


def _topk(scores_t, tl=256):
    nh2, nk, t = scores_t.shape
    nsel = PEER_HEADS * PEER_TOPK
    out = pl.BlockSpec((tl, nsel), lambda i: (i, 0))
    return pl.pallas_call(
        _topk_kernel,
        grid=(t // tl,),
        in_specs=[pl.BlockSpec((nh2, nk, tl), lambda i: (0, 0, i))],
        out_specs=[out, out, out],
        out_shape=[jax.ShapeDtypeStruct((t, nsel), jnp.int32), jax.ShapeDtypeStruct((t, nsel), jnp.int32),
                   jax.ShapeDtypeStruct((t, nsel), F32)],
        scratch_shapes=[pltpu.VMEM((_N_CAND_PAD, tl), F32), pltpu.VMEM((_N_CAND_PAD, tl), F32),
                        pltpu.VMEM((nsel, tl), jnp.int32), pltpu.VMEM((nsel, tl), F32)],
        compiler_params=pltpu.CompilerParams(dimension_semantics=("arbitrary",), vmem_limit_bytes=VMEM_LIMIT),
        name="topk",
    )(scores_t)


def _peer_kernel(h2_ref, u_ref, v_ref, isel_ref, jsel_ref, gsel_ref, x1_ref, gate_ref, gf_ref, o_ref,
                 gs_ref, *, tm, te):
    e = pl.program_id(1)
    n_i = te // PEER_KEYS

    @pl.when(e == 0)
    def _():
        o_ref[...] = jnp.zeros_like(o_ref)
        iota0 = lax.broadcasted_iota(jnp.int32, (PEER_KEYS, LANES), 0)

        def per_token(t, carry):
            isel = isel_ref[pl.ds(t, 1), :]
            jsel = jsel_ref[pl.ds(t, 1), :]
            g = gsel_ref[pl.ds(t, 1), :]
            gi = jnp.where(iota0 == isel, g, 0.0).astype(BF16)
            oj = jnp.where(iota0 == jsel, 1.0, 0.0).astype(BF16)
            gs_ref[pl.ds(t * GS_PITCH, PEER_KEYS), :] = _nt(gi, oj)
            return carry

        lax.fori_loop(0, tm, per_token, 0, unroll=64)

    a = _nt(h2_ref[...], u_ref[...])
    act = 0.5 * a * (1.0 + lax.erf(a * (1.0 / math.sqrt(2.0))))
    parts = []
    for ii in range(n_i):
        gm = gs_ref[pl.ds(e * n_i + ii, tm, stride=GS_PITCH), :]
        parts.append((act[:, ii * PEER_KEYS:(ii + 1) * PEER_KEYS] * gm).astype(BF16))
    w = jnp.concatenate(parts, axis=1)
    o_ref[...] += jnp.dot(w, v_ref[...], preferred_element_type=F32)

    @pl.when(e == pl.num_programs(1) - 1)
    def _():
        x2 = x1_ref[...] + gate_ref[0] * o_ref[...]
        o_ref[...] = x2 * _rms_scale(x2) * gf_ref[...]


def _peer(h2, u, v, isel, jsel, gsel, x1, gate, g_final, seq, tm=512, te=1024):
    t, d = h2.shape
    n_e = u.shape[0]
    per_batch = seq // tm
    tok = lambda w: pl.BlockSpec((tm, w), lambda i, e: (i, 0))
    exp = pl.BlockSpec((te, d), lambda i, e: (e, 0))
    return pl.pallas_call(
        functools.partial(_peer_kernel, tm=tm, te=te),
        grid=(t // tm, n_e // te),
        in_specs=[tok(d), exp, exp, tok(isel.shape[1]), tok(isel.shape[1]), tok(isel.shape[1]),
                  pl.BlockSpec((tm, d), lambda i, e: (i, 0), pipeline_mode=pl.Buffered(1)),
                  pl.BlockSpec((1, 1, d), lambda i, e: (i // per_batch, 0, 0)),
                  pl.BlockSpec((1, d), lambda i, e: (0, 0))],
        out_specs=tok(d),
        out_shape=jax.ShapeDtypeStruct((t, d), F32),
        scratch_shapes=[pltpu.VMEM((tm * GS_PITCH, LANES), F32)],
        compiler_params=pltpu.CompilerParams(dimension_semantics=("arbitrary", "arbitrary"), vmem_limit_bytes=VMEM_LIMIT),
        name="peer",
    )(h2, u, v, isel, jsel, gsel, x1, gate, g_final)


def kernel(x, c, w_ada, b_ada, w_in, w_gla_gate_up, b_gla_gate, g_sb_norm, g_gla_norm, w_out, w_peer_q,
           peer_keys_a, peer_keys_b, peer_u, peer_v, g_final):
    b, s, d = x.shape
    depth = w_ada.shape[0]
    sbw = SB_HEADS * SB_HEAD_DIM
    gkw = GLA_HEADS * GLA_DK
    gvw = GLA_HEADS * GLA_DV
    lr0 = 3 * sbw + 2 * gkw + gvw
    c_pad = jnp.pad(c, ((0, 8 - b), (0, 0)))
    for l in range(depth):
        mod = _ada(c_pad, w_ada[l], b_ada[l][None, :])[:b]
        shift1, scale1, gate1, shift2, scale2, gate2 = (mod[:, None, i * d:(i + 1) * d] for i in range(6))

        w_l = w_in[l]
        w_main = jnp.concatenate([w_l[:, :lr0], w_l[:, lr0 + GLA_GATE_RANK:]], axis=1).astype(BF16)
        w_lr = jnp.pad(w_l[:, lr0:lr0 + GLA_GATE_RANK], ((0, 0), (0, LANES - GLA_GATE_RANK))).astype(BF16)
        w_up = jnp.pad(w_gla_gate_up[l], ((0, LANES - GLA_GATE_RANK), (0, 0)))
        sbq, sbk, sbv, gq, gk, gv, gr, la = _inproj(x, shift1, scale1, w_main, w_lr, w_up, b_gla_gate[l][None, :])

        sb_out = _sb_attention(sbq, sbk, sbv, g_sb_norm[l][None, :])
        gla_out = _gla(gq, gk, gv, la, gr, g_gla_norm[l][None, :])

        x1, h2, scores_t = _mix(sb_out, gla_out, x, gate1, shift2, scale2, w_out[l].astype(BF16),
                                w_peer_q[l].astype(BF16), peer_keys_a[l].astype(BF16), peer_keys_b[l].astype(BF16))
        isel, jsel, gsel = _topk(scores_t)
        x = _peer(h2.reshape(b * s, d), peer_u[l].astype(BF16), peer_v[l].astype(BF16), isel, jsel, gsel,
                  x1.reshape(b * s, d), gate2, g_final[None, :], s).reshape(b, s, d)
    return x
```

```python
import functools
import math

import jax
import jax.numpy as jnp
from jax import lax
from jax.experimental import pallas as pl
from jax.experimental.pallas import tpu as pltpu

F32 = jnp.float32
BF16 = jnp.bfloat16
HIGHEST = lax.Precision.HIGHEST

EPS = 1e-6
LANES = 128
SB_HEADS = 8
SB_HEAD_DIM = 64
GLA_HEADS = 4
GLA_DK = 64
GLA_DV = 128
GLA_GATE_RANK = 16
GLA_GATE_TAU = 16.0
GLA_CHUNK = 64
GLA_GROUP = 4
GLA_SUB = 16
PEER_HEADS = 8
PEER_KEYS = 128
PEER_TOPK = 16
PEER_HALF = 128
SB_EXP_ZERO = -104.0
SB_SPAN = 3
GS_PITCH = PEER_KEYS + 4

VMEM_LIMIT = 56 * 1024 * 1024


def _nt(a, b):
    return lax.dot_general(a, b, (((1,), (1,)), ((), ())), preferred_element_type=F32)


def _tn(a, b):
    return lax.dot_general(a, b, (((0,), (0,)), ((), ())), preferred_element_type=F32)


def _log_sigmoid(z):
    return jnp.minimum(z, 0.0) - jnp.log(1.0 + jnp.exp(-jnp.abs(z)))


def _rms_scale(x):
    return lax.rsqrt(jnp.mean(x * x, axis=-1, keepdims=True) + EPS)


def _ada_kernel(c_ref, w_ref, b_ref, o_ref):
    c = c_ref[...]
    c_act = c * jax.nn.sigmoid(c)
    o_ref[...] = jnp.dot(c_act, w_ref[...], precision=HIGHEST, preferred_element_type=F32) + b_ref[...]


def _ada(c_pad, w, b, tn=1536):
    rows, d = c_pad.shape
    n = w.shape[1]
    return pl.pallas_call(
        _ada_kernel,
        grid=(n // tn,),
        in_specs=[pl.BlockSpec((rows, d), lambda j: (0, 0)),
                  pl.BlockSpec((d, tn), lambda j: (0, j)),
                  pl.BlockSpec((1, tn), lambda j: (0, j))],
        out_specs=pl.BlockSpec((rows, tn), lambda j: (0, j)),
        out_shape=jax.ShapeDtypeStruct((rows, n), F32),
        compiler_params=pltpu.CompilerParams(dimension_semantics=("arbitrary",), vmem_limit_bytes=VMEM_LIMIT),
        name="ada",
    )(c_pad, w, b)


def _inproj_kernel(x_ref, shift_ref, scale_ref, wm_ref, wlr_ref, wup_ref, bg_ref,
                   sbq_ref, sbk_ref, sbv_ref, gq_ref, gk_ref, gv_ref, gr_ref, la_ref):
    x = x_ref[0]
    h = (x * _rms_scale(x)) * (1.0 + scale_ref[0]) + shift_ref[0]
    hb = h.astype(BF16)
    sbw = SB_HEADS * SB_HEAD_DIM
    gkw = GLA_HEADS * GLA_DK
    gvw = GLA_HEADS * GLA_DV
    outs = ((sbq_ref, sbw, 0.125), (sbk_ref, sbw, None), (sbv_ref, sbw, None),
            (gq_ref, gkw, 0.125), (gk_ref, gkw, None), (gv_ref, gvw, None), (gr_ref, gvw, None))
    off = 0
    for ref, width, mul in outs:
        p = jnp.dot(hb, wm_ref[:, off:off + width], preferred_element_type=F32)
        if mul is not None:
            p = p * mul
        ref[0] = p.astype(ref.dtype)
        off += width
    g_lr = jnp.dot(hb, wlr_ref[...], preferred_element_type=F32)
    u = jnp.dot(g_lr, wup_ref[...], precision=HIGHEST, preferred_element_type=F32) + bg_ref[...]
    la_ref[0] = _log_sigmoid(u) * (1.0 / GLA_GATE_TAU)


def _inproj(x, shift, scale, w_main, w_lr, w_up, b_gate, tm=512):
    b, s, d = x.shape
    sbw = SB_HEADS * SB_HEAD_DIM
    gkw = GLA_HEADS * GLA_DK
    gvw = GLA_HEADS * GLA_DV
    widths = (sbw, sbw, sbw, gkw, gkw, gvw, gvw)
    tok = lambda w: pl.BlockSpec((1, tm, w), lambda bi, i: (bi, i, 0))
    full = lambda a: pl.BlockSpec(a.shape, lambda bi, i: (0,) * a.ndim)
    mod = pl.BlockSpec((1, 1, d), lambda bi, i: (bi, 0, 0))
    return pl.pallas_call(
        _inproj_kernel,
        grid=(b, s // tm),
        in_specs=[tok(d), mod, mod, full(w_main), full(w_lr), full(w_up), full(b_gate)],
        out_specs=[tok(w) for w in widths] + [tok(gkw)],
        out_shape=[jax.ShapeDtypeStruct((b, s, w), BF16) for w in widths] + [jax.ShapeDtypeStruct((b, s, gkw), F32)],
        compiler_params=pltpu.CompilerParams(dimension_semantics=("arbitrary", "arbitrary"), vmem_limit_bytes=VMEM_LIMIT),
        name="inproj",
    )(x, shift, scale, w_main, w_lr, w_up, b_gate)


def _sb_kernel(q_ref, k_ref, v_ref, g_ref, o_ref, *, tq, nq):
    gi = pl.program_id(2)
    lane = lax.broadcasted_iota(jnp.int32, (tq, LANES), 1)
    row = lax.broadcasted_iota(jnp.int32, (tq, tq), 0)
    col = lax.broadcasted_iota(jnp.int32, (tq, tq), 1)
    later = jnp.where(row > col, 1.0, 0.0).astype(BF16)
    later2 = jnp.concatenate([later, later], axis=0)

    n_heads = LANES // SB_HEAD_DIM
    tqs = nq * tq
    rows = n_heads * tqs
    in_head = [(lane >= h * SB_HEAD_DIM) & (lane < (h + 1) * SB_HEAD_DIM) for h in range(n_heads)]
    nspan = SB_SPAN - 1 + nq
    span = nspan * tq
    q_row = lax.broadcasted_iota(jnp.int32, (rows, span), 0) & (tqs - 1)
    k_col = lax.broadcasted_iota(jnp.int32, (rows, span), 1)

    def suffix_in_blocks(log_keep, nblk):
        parts = []
        for b in range(nblk):
            lk = log_keep[:, b * tq:(b + 1) * tq]
            hi = lk.astype(BF16)
            parts.append(jnp.concatenate([hi, (lk - hi.astype(F32)).astype(BF16)], axis=1))
        cs = jnp.dot(jnp.concatenate(parts, axis=0), later2, preferred_element_type=F32)
        return [cs[b * rows:(b + 1) * rows] for b in range(nblk)]

    def first_span(q2, i):
        b0 = jnp.maximum(i - (SB_SPAN - 1), 0)
        start = pl.multiple_of(b0 * tq, tq)
        z = _nt(q2, k_ref[0, pl.ds(start, span), :])
        log_beta = _log_sigmoid(z)
        valid = k_col + (b0 - i) * tq < q_row
        log_keep = jnp.where(valid, log_beta - z, 0.0)
        within = suffix_in_blocks(log_keep, nspan)
        r_run = jnp.zeros((rows, 1), F32)
        ws = [None] * nspan
        for b in reversed(range(nspan)):
            blk = slice(b * tq, (b + 1) * tq)
            w = jnp.exp(log_beta[:, blk] + within[b] + r_run)
            ws[b] = jnp.where(valid[:, blk], w, 0.0).astype(BF16)
            r_run = r_run + jnp.sum(log_keep[:, blk], axis=-1, keepdims=True)
        acc = jnp.dot(jnp.concatenate(ws, axis=1), v_ref[0, pl.ds(start, span), :], preferred_element_type=F32)
        return b0 - 1, r_run, acc

    def one_block(q2, j, r_run, acc):
        start = pl.multiple_of(j * tq, tq)
        z = _nt(q2, k_ref[0, pl.ds(start, tq), :])
        log_beta = _log_sigmoid(z)
        log_keep = log_beta - z
        w = jnp.exp(log_beta + suffix_in_blocks(log_keep, 1)[0] + r_run)
        acc = acc + jnp.dot(w.astype(BF16), v_ref[0, pl.ds(start, tq), :], preferred_element_type=F32)
        return r_run + jnp.sum(log_keep, axis=-1, keepdims=True), acc

    q = q_ref[0]
    lane_q = lax.broadcasted_iota(jnp.int32, q.shape, 1)
    q2 = jnp.concatenate(
        [jnp.where((lane_q >= h * SB_HEAD_DIM) & (lane_q < (h + 1) * SB_HEAD_DIM), q, jnp.zeros_like(q))
         for h in range(n_heads)], axis=0)
    j0, r0, acc0 = first_span(q2, gi * nq)

    def cond(carry):
        j, r_run, _ = carry
        return jnp.logical_and(j >= 0, jnp.max(r_run) > SB_EXP_ZERO)

    def body(carry):
        j, r_run, acc = carry
        r_run, acc = one_block(q2, j, r_run, acc)
        return j - 1, r_run, acc

    _, _, acc = lax.while_loop(cond, body, (j0, r0, acc0))

    for qb in range(nq):
        rows_qb = slice(qb * tq, (qb + 1) * tq)
        o = jnp.where(in_head[0], acc[:tqs][rows_qb], acc[tqs:][rows_qb])
        o2 = o * o
        ms = jnp.zeros_like(o)
        for h in range(n_heads):
            ms_h = jnp.sum(jnp.where(in_head[h], o2, 0.0), axis=-1, keepdims=True) * (1.0 / SB_HEAD_DIM)
            ms = jnp.where(in_head[h], ms_h, ms)
        o_ref[0, rows_qb, :] = (o * lax.rsqrt(ms + EPS) * g_ref[...]).astype(o_ref.dtype)


def _sb_attention(q, k, v, gain, tq=128, nq=2):
    b, s, w = q.shape
    pairs = w // LANES
    return pl.pallas_call(
        functools.partial(_sb_kernel, tq=tq, nq=nq),
        grid=(b, pairs, s // (tq * nq)),
        in_specs=[pl.BlockSpec((1, tq * nq, LANES), lambda bi, p, i: (bi, i, p)),
                  pl.BlockSpec((1, s, LANES), lambda bi, p, i: (bi, 0, p)),
                  pl.BlockSpec((1, s, LANES), lambda bi, p, i: (bi, 0, p)),
                  pl.BlockSpec((1, LANES), lambda bi, p, i: (0, p))],
        out_specs=pl.BlockSpec((1, tq * nq, LANES), lambda bi, p, i: (bi, i, p)),
        out_shape=jax.ShapeDtypeStruct((b, s, w), BF16),
        compiler_params=pltpu.CompilerParams(dimension_semantics=("arbitrary",) * 3, vmem_limit_bytes=VMEM_LIMIT),
        name="sb",
    )(q, k, v, gain)


def _gla_kernel(q_ref, k_ref, v_ref, la_ref, r_ref, g_ref, o_ref, st_ref):
    c = GLA_CHUNK
    nsub = c // GLA_SUB
    n = GLA_GROUP * c
    w = la_ref.shape[2]

    @pl.when(pl.program_id(1) == 0)
    def _():
        st_ref[...] = jnp.zeros_like(st_ref)

    la = la_ref[0]
    row = lax.broadcasted_iota(jnp.int32, (n, n), 0)
    col = lax.broadcasted_iota(jnp.int32, (n, n), 1)
    incl = jnp.where(col <= row, 1.0, 0.0).astype(F32)
    g_cum = jnp.dot(incl, la, precision=HIGHEST, preferred_element_type=F32)
    g_exc = g_cum - la
    g_last = g_cum[n - 1:n, :]

    def rows_from(starts, count):
        return jnp.concatenate([jnp.broadcast_to(g_exc[s0:s0 + 1, :], (count, w)) for s0 in starts], axis=0)

    sub_ref = rows_from(range(0, n, GLA_SUB), GLA_SUB)
    chunk_ref = rows_from(range(0, n, c), c)
    e_q_sub = jnp.exp(g_cum - sub_ref)
    e_q_chunk = jnp.exp(g_cum - chunk_ref)
    e_inter = jnp.exp(g_cum)
    e_k_last = jnp.exp(g_last - g_cum)
    e_last = jnp.exp(g_last)
    e_k_sub = [jnp.exp(jnp.minimum(rows_from([c * ch + GLA_SUB * sb for ch in range(GLA_GROUP)], c) - g_cum, 0.0))
               for sb in range(1, nsub)]
    e_k_chunk = [jnp.exp(jnp.minimum(g_exc[c * ch:c * ch + 1, :] - g_cum, 0.0)) for ch in range(1, GLA_GROUP)]

    row_w = lax.broadcasted_iota(jnp.int32, (n, w), 0)
    seg_cum, seg_tot = la, la
    e_q_half, e_k_half = [], []
    half = 1
    while half < GLA_SUB:
        e_q_half.append(jnp.exp(seg_cum))
        e_k_half.append(jnp.exp(seg_tot - seg_cum))
        upper = (row_w & half) != 0
        partner_tot = jnp.where(upper, pltpu.roll(seg_tot, half, axis=0), pltpu.roll(seg_tot, n - half, axis=0))
        seg_cum = seg_cum + jnp.where(upper, partner_tot, 0.0)
        seg_tot = seg_tot + partner_tot
        half *= 2

    rowk = lax.broadcasted_iota(jnp.int32, (n, GLA_DK), 0)
    c_shift = c.bit_length() - 1
    s_shift = GLA_SUB.bit_length() - 1
    sub_of_row = (rowk & (c - 1)) >> s_shift
    chunk_of_row = rowk >> c_shift
    same_chunk = (row >> c_shift) == (col >> c_shift)
    earlier_sub = same_chunk & ((col >> s_shift) < (row >> s_shift))
    earlier = (col >> c_shift) < (row >> c_shift)
    half_masks = []
    for lvl in range(len(e_q_half)):
        half_masks.append(((row >> (lvl + 1)) == (col >> (lvl + 1))) & ((row & (1 << lvl)) != 0)
                          & ((col & (1 << lvl)) == 0))
    diagonal = row == col

    q_all = q_ref[0].astype(F32)
    k_all = k_ref[0].astype(F32)
    v_all = v_ref[0]
    gate = r_ref[0].astype(F32)
    gate = gate * jax.nn.sigmoid(gate)
    for h in range(GLA_HEADS):
        ks = slice(h * GLA_DK, (h + 1) * GLA_DK)
        vs = slice(h * GLA_DV, (h + 1) * GLA_DV)
        q = q_all[:, ks]
        k = k_all[:, ks]
        v = v_all[:, vs]
        scores = jnp.where(diagonal, _nt(q.astype(BF16), k.astype(BF16)), 0.0)
        for lvl in range(len(e_q_half)):
            bit = 1 << lvl
            q_up = jnp.where((rowk & bit) != 0, q * e_q_half[lvl][:, ks], 0.0).astype(BF16)
            k_lo = jnp.where((rowk & bit) == 0, k * e_k_half[lvl][:, ks], 0.0).astype(BF16)
            scores = jnp.where(half_masks[lvl], _nt(q_up, k_lo), scores)
        qt = q * e_q_sub[:, ks]
        q_hat = jnp.concatenate([jnp.where(sub_of_row == sb, qt, 0.0) for sb in range(1, nsub)], axis=1).astype(BF16)
        k_hat = jnp.concatenate(
            [jnp.where((rowk & (c - 1)) < GLA_SUB * sb, k * e_k_sub[sb - 1][:, ks], 0.0) for sb in range(1, nsub)],
            axis=1).astype(BF16)
        scores = jnp.where(earlier_sub, _nt(q_hat, k_hat), scores)
        if GLA_GROUP > 1:
            qc = q * e_q_chunk[:, ks]
            qc_hat = jnp.concatenate([jnp.where(chunk_of_row == ch, qc, 0.0) for ch in range(1, GLA_GROUP)],
                                     axis=1).astype(BF16)
            kc_hat = jnp.concatenate(
                [jnp.where(chunk_of_row < ch, k * e_k_chunk[ch - 1][:, ks], 0.0) for ch in range(1, GLA_GROUP)],
                axis=1).astype(BF16)
            scores = jnp.where(earlier, _nt(qc_hat, kc_hat), scores)
        o_intra = jnp.dot(scores.astype(BF16), v, preferred_element_type=F32)
        st = st_ref[h]
        o_inter = _nt((q * e_inter[:, ks]).astype(BF16), st.astype(BF16))
        k_dec = (k * e_k_last[:, ks]).astype(BF16)
        st_ref[h] = st * e_last[:, ks] + _tn(v, k_dec)
        o = o_intra + o_inter
        o = o * _rms_scale(o) * g_ref[:, vs] * gate[:, vs]
        o_ref[0, :, vs] = o.astype(o_ref.dtype)


def _gla(q, k, v, la, r, gain):
    b, s, _ = q.shape
    c = GLA_GROUP * GLA_CHUNK
    kw = GLA_HEADS * GLA_DK
    vw = GLA_HEADS * GLA_DV
    tok = lambda w: pl.BlockSpec((1, c, w), lambda bi, i: (bi, i, 0))
    return pl.pallas_call(
        _gla_kernel,
        grid=(b, s // c),
        in_specs=[tok(kw), tok(kw), tok(vw), tok(kw), tok(vw), pl.BlockSpec((1, vw), lambda bi, i: (0, 0))],
        out_specs=tok(vw),
        out_shape=jax.ShapeDtypeStruct((b, s, vw), BF16),
        scratch_shapes=[pltpu.VMEM((GLA_HEADS, GLA_DV, GLA_DK), F32)],
        compiler_params=pltpu.CompilerParams(dimension_semantics=("arbitrary", "arbitrary"), vmem_limit_bytes=VMEM_LIMIT),
        name="gla",
    )(q, k, v, la, r, gain)


def _mix_kernel(sb_ref, gla_ref, x_ref, gate_ref, shift_ref, scale_ref, wo_ref, wq_ref, ka_ref, kb_ref,
                x1_ref, h2_ref, st_ref):
    half = sb_ref.shape[2]
    mixed = (jnp.dot(sb_ref[0], wo_ref[:half, :], preferred_element_type=F32)
             + jnp.dot(gla_ref[0], wo_ref[half:, :], preferred_element_type=F32))
    x1 = x_ref[0] + gate_ref[0] * mixed
    x1_ref[0] = x1
    h2 = ((x1 * _rms_scale(x1)) * (1.0 + scale_ref[0]) + shift_ref[0]).astype(BF16)
    h2_ref[0] = h2
    q = jnp.dot(h2, wq_ref[...], preferred_element_type=F32).astype(BF16)
    for h in range(PEER_HEADS):
        base = 2 * PEER_HALF * h
        st_ref[2 * h] = _nt(ka_ref[h], q[:, base:base + PEER_HALF])
        st_ref[2 * h + 1] = _nt(kb_ref[h], q[:, base + PEER_HALF:base + 2 * PEER_HALF])


def _mix(sb, gla, x, gate, shift, scale, w_out, w_q, keys_a, keys_b, tm=256):
    b, s, d = x.shape
    ns = s // tm
    tok = lambda w: pl.BlockSpec((1, tm, w), lambda bi, i: (bi, i, 0))
    full = lambda a: pl.BlockSpec(a.shape, lambda bi, i: (0,) * a.ndim)
    mod = pl.BlockSpec((1, 1, d), lambda bi, i: (bi, 0, 0))
    return pl.pallas_call(
        _mix_kernel,
        grid=(b, ns),
        in_specs=[tok(sb.shape[2]), tok(gla.shape[2]), tok(d), mod, mod, mod,
                  full(w_out), full(w_q), full(keys_a), full(keys_b)],
        out_specs=[tok(d), tok(d),
                   pl.BlockSpec((2 * PEER_HEADS, PEER_KEYS, tm), lambda bi, i: (0, 0, bi * ns + i))],
        out_shape=[jax.ShapeDtypeStruct((b, s, d), F32), jax.ShapeDtypeStruct((b, s, d), BF16),
                   jax.ShapeDtypeStruct((2 * PEER_HEADS, PEER_KEYS, b * s), F32)],
        compiler_params=pltpu.CompilerParams(dimension_semantics=("arbitrary", "arbitrary"), vmem_limit_bytes=VMEM_LIMIT),
        name="mix",
    )(sb, gla, x, gate, shift, scale, w_out, w_q, keys_a, keys_b)


def _sort_network(n):
    pairs = []
    p = 1
    while p < n:
        k = p
        while k >= 1:
            for j in range(k % p, n - k, 2 * k):
                for i in range(min(k, n - j - k)):
                    if (i + j) // (2 * p) == (i + j + k) // (2 * p):
                        pairs.append((i + j, i + j + k))
            k //= 2
        p *= 2
    return pairs


SUBLANES = 8
_PAIR_COUNTS = tuple(PEER_TOPK // (a + 1) for a in range(PEER_TOPK))
_N_CAND = sum(_PAIR_COUNTS)
_N_CAND_PAD = -(-_N_CAND // SUBLANES) * SUBLANES


def _top_list(s, k):
    n = s.shape[0]
    depth = n // SUBLANES
    vals = [s[SUBLANES * v:SUBLANES * (v + 1), :] for v in range(depth)]
    base = lax.broadcasted_iota(jnp.int32, vals[0].shape, 0).astype(F32)
    idxs = [base + float(SUBLANES * v) for v in range(depth)]
    for lo, hi in _sort_network(depth):
        swap = (vals[hi] > vals[lo]) | ((vals[hi] == vals[lo]) & (idxs[hi] < idxs[lo]))
        vals[lo], vals[hi] = jnp.where(swap, vals[hi], vals[lo]), jnp.where(swap, vals[lo], vals[hi])
        idxs[lo], idxs[hi] = jnp.where(swap, idxs[hi], idxs[lo]), jnp.where(swap, idxs[lo], idxs[hi])
    top_v, top_i = [], []
    for step in range(k):
        m = jnp.max(vals[0], axis=0, keepdims=True)
        idx = jnp.min(jnp.where(vals[0] == m, idxs[0], float(n)), axis=0, keepdims=True)
        top_v.append(m)
        top_i.append(idx)
        hit = idxs[0] == idx
        live = min(depth, k - step)
        for p in range(live - 1):
            vals[p] = jnp.where(hit, vals[p + 1], vals[p])
            idxs[p] = jnp.where(hit, idxs[p + 1], idxs[p])
        if live == depth:
            vals[depth - 1] = jnp.where(hit, -jnp.inf, vals[depth - 1])
    return top_v, top_i


def _topk_kernel(s_ref, i_ref, j_ref, g_ref, cand_ref, code_ref, oc_ref, og_ref):
    tl = s_ref.shape[2]
    k = PEER_TOPK
    iota_cand = lax.broadcasted_iota(jnp.int32, (_N_CAND_PAD, tl), 0).astype(F32)

    def head(h, carry):
        top_v, top_i = _top_list(jnp.concatenate([s_ref[2 * h], s_ref[2 * h + 1]], axis=1), k)
        v1 = [v[:, :tl] for v in top_v]
        i1 = [i[:, :tl] for i in top_i]
        v2_all = jnp.concatenate([v[:, tl:] for v in top_v], axis=0)
        i2_all = jnp.concatenate([i[:, tl:] for i in top_i], axis=0)
        cand_ref[...] = jnp.full((_N_CAND_PAD, tl), -jnp.inf, F32)
        code_ref[...] = jnp.zeros((_N_CAND_PAD, tl), F32)
        off = 0
        for a, nb in enumerate(_PAIR_COUNTS):
            cand_ref[off:off + nb, :] = v1[a] + v2_all[:nb]
            code_ref[off:off + nb, :] = i1[a] * float(PEER_KEYS) + i2_all[:nb]
            off += nb
        cand = cand_ref[...]
        code = code_ref[...]
        tops, sel = [], []
        for _ in range(k):
            m = jnp.max(cand, axis=0, keepdims=True)
            pos = jnp.min(jnp.where(cand == m, iota_cand, float(_N_CAND_PAD)), axis=0, keepdims=True)
            hit = iota_cand == pos
            tops.append(m)
            sel.append(jnp.sum(jnp.where(hit, code, 0.0), axis=0, keepdims=True))
            cand = jnp.where(hit, -jnp.inf, cand)
        top = jnp.concatenate(tops, axis=0)
        e = jnp.exp(top - top[0:1])
        gates = e / jnp.sum(e, axis=0, keepdims=True)
        rows = pl.ds(pl.multiple_of(h * k, k), k)
        oc_ref[rows, :] = jnp.concatenate(sel, axis=0).astype(jnp.int32)
        og_ref[rows, :] = gates
        return carry

    lax.fori_loop(0, PEER_HEADS, head, 0)
    shift = PEER_KEYS.bit_length() - 1
    for t0 in range(0, tl, LANES):
        code = oc_ref[:, t0:t0 + LANES].T
        i_ref[t0:t0 + LANES, :] = code >> shift
        j_ref[t0:t0 + LANES, :] = code & (PEER_KEYS - 1)
        g_ref[t0:t0 + LANES, :] = og_ref[:, t0:t0 + LANES].T


def _topk(scores_t, tl=256):
    nh2, nk, t = scores_t.shape
    nsel = PEER_HEADS * PEER_TOPK
    out = pl.BlockSpec((tl, nsel), lambda i: (i, 0))
    return pl.pallas_call(
        _topk_kernel,
        grid=(t // tl,),
        in_specs=[pl.BlockSpec((nh2, nk, tl), lambda i: (0, 0, i))],
        out_specs=[out, out, out],
        out_shape=[jax.ShapeDtypeStruct((t, nsel), jnp.int32), jax.ShapeDtypeStruct((t, nsel), jnp.int32),
                   jax.ShapeDtypeStruct((t, nsel), F32)],
        scratch_shapes=[pltpu.VMEM((_N_CAND_PAD, tl), F32), pltpu.VMEM((_N_CAND_PAD, tl), F32),
                        pltpu.VMEM((nsel, tl), jnp.int32), pltpu.VMEM((nsel, tl), F32)],
        compiler_params=pltpu.CompilerParams(dimension_semantics=("arbitrary",), vmem_limit_bytes=VMEM_LIMIT),
        name="topk",
    )(scores_t)


def _peer_kernel(h2_ref, u_ref, v_ref, isel_ref, jsel_ref, gsel_ref, x1_ref, gate_ref, gf_ref, o_ref,
                 gs_ref, *, tm, te):
    e = pl.program_id(1)
    n_i = te // PEER_KEYS

    @pl.when(e == 0)
    def _():
        o_ref[...] = jnp.zeros_like(o_ref)
        iota0 = lax.broadcasted_iota(jnp.int32, (PEER_KEYS, LANES), 0)

        def per_token(t, carry):
            isel = isel_ref[pl.ds(t, 1), :]
            jsel = jsel_ref[pl.ds(t, 1), :]
            g = gsel_ref[pl.ds(t, 1), :]
            gi = jnp.where(iota0 == isel, g, 0.0).astype(BF16)
            oj = jnp.where(iota0 == jsel, 1.0, 0.0).astype(BF16)
            gs_ref[pl.ds(t * GS_PITCH, PEER_KEYS), :] = _nt(gi, oj)
            return carry

        lax.fori_loop(0, tm, per_token, 0, unroll=64)

    a = _nt(h2_ref[...], u_ref[...])
    act = 0.5 * a * (1.0 + lax.erf(a * (1.0 / math.sqrt(2.0))))
    parts = []
    for ii in range(n_i):
        gm = gs_ref[pl.ds(e * n_i + ii, tm, stride=GS_PITCH), :]
        parts.append((act[:, ii * PEER_KEYS:(ii + 1) * PEER_KEYS] * gm).astype(BF16))
    w = jnp.concatenate(parts, axis=1)
    o_ref[...] += jnp.dot(w, v_ref[...], preferred_element_type=F32)

    @pl.when(e == pl.num_programs(1) - 1)
    def _():
        x2 = x1_ref[...] + gate_ref[0] * o_ref[...]
        o_ref[...] = x2 * _rms_scale(x2) * gf_ref[...]


def _peer(h2, u, v, isel, jsel, gsel, x1, gate, g_final, seq, tm=512, te=1024):
    t, d = h2.shape
    n_e = u.shape[0]
    per_batch = seq // tm
    tok = lambda w: pl.BlockSpec((tm, w), lambda i, e: (i, 0))
    exp = pl.BlockSpec((te, d), lambda i, e: (e, 0))
    return pl.pallas_call(
        functools.partial(_peer_kernel, tm=tm, te=te),
        grid=(t // tm, n_e // te),
        in_specs=[tok(d), exp, exp, tok(isel.shape[1]), tok(isel.shape[1]), tok(isel.shape[1]),
                  pl.BlockSpec((tm, d), lambda i, e: (i, 0), pipeline_mode=pl.Buffered(1)),
                  pl.BlockSpec((1, 1, d), lambda i, e: (i // per_batch, 0, 0)),
                  pl.BlockSpec((1, d), lambda i, e: (0, 0))],
        out_specs=tok(d),
        out_shape=jax.ShapeDtypeStruct((t, d), F32),
        scratch_shapes=[pltpu.VMEM((tm * GS_PITCH, LANES), F32)],
        compiler_params=pltpu.CompilerParams(dimension_semantics=("arbitrary", "arbitrary"), vmem_limit_bytes=VMEM_LIMIT),
        name="peer",
    )(h2, u, v, isel, jsel, gsel, x1, gate, g_final)


def kernel(x, c, w_ada, b_ada, w_in, w_gla_gate_up, b_gla_gate, g_sb_norm, g_gla_norm, w_out, w_peer_q,
           peer_keys_a, peer_keys_b, peer_u, peer_v, g_final):
    b, s, d = x.shape
    depth = w_ada.shape[0]
    sbw = SB_HEADS * SB_HEAD_DIM
    gkw = GLA_HEADS * GLA_DK
    gvw = GLA_HEADS * GLA_DV
    lr0 = 3 * sbw + 2 * gkw + gvw
    c_pad = jnp.pad(c, ((0, 8 - b), (0, 0)))
    for l in range(depth):
        mod = _ada(c_pad, w_ada[l], b_ada[l][None, :])[:b]
        shift1, scale1, gate1, shift2, scale2, gate2 = (mod[:, None, i * d:(i + 1) * d] for i in range(6))

        w_l = w_in[l]
        w_main = jnp.concatenate([w_l[:, :lr0], w_l[:, lr0 + GLA_GATE_RANK:]], axis=1).astype(BF16)
        w_lr = jnp.pad(w_l[:, lr0:lr0 + GLA_GATE_RANK], ((0, 0), (0, LANES - GLA_GATE_RANK))).astype(BF16)
        w_up = jnp.pad(w_gla_gate_up[l], ((0, LANES - GLA_GATE_RANK), (0, 0)))
        sbq, sbk, sbv, gq, gk, gv, gr, la = _inproj(x, shift1, scale1, w_main, w_lr, w_up, b_gla_gate[l][None, :])

        sb_out = _sb_attention(sbq, sbk, sbv, g_sb_norm[l][None, :])
        gla_out = _gla(gq, gk, gv, la, gr, g_gla_norm[l][None, :])

        x1, h2, scores_t = _mix(sb_out, gla_out, x, gate1, shift2, scale2, w_out[l].astype(BF16),
                                w_peer_q[l].astype(BF16), peer_keys_a[l].astype(BF16), peer_keys_b[l].astype(BF16))
        isel, jsel, gsel = _topk(scores_t)
        x = _peer(h2.reshape(b * s, d), peer_u[l].astype(BF16), peer_v[l].astype(BF16), isel, jsel, gsel,
                  x1.reshape(b * s, d), gate2, g_final[None, :], s).reshape(b, s, d)
    return x
```

```python
import functools
import math

import jax
import jax.numpy as jnp
from jax import lax
from jax.experimental import pallas as pl
from jax.experimental.pallas import tpu as pltpu

F32 = jnp.float32
BF16 = jnp.bfloat16
HIGHEST = lax.Precision.HIGHEST

EPS = 1e-6
LANES = 128
SB_HEADS = 8
SB_HEAD_DIM = 64
GLA_HEADS = 4
GLA_DK = 64
GLA_DV = 128
GLA_GATE_RANK = 16
GLA_GATE_TAU = 16.0
GLA_CHUNK = 64
GLA_GROUP = 4
GLA_SUB = 16
PEER_HEADS = 8
PEER_KEYS = 128
PEER_TOPK = 16
PEER_HALF = 128
SB_EXP_ZERO = -104.0
SB_SPAN = 3
GS_PITCH = PEER_KEYS + 4

VMEM_LIMIT = 56 * 1024 * 1024


def _nt(a, b):
    return lax.dot_general(a, b, (((1,), (1,)), ((), ())), preferred_element_type=F32)


def _tn(a, b):
    return lax.dot_general(a, b, (((0,), (0,)), ((), ())), preferred_element_type=F32)


def _log_sigmoid(z):
    return jnp.minimum(z, 0.0) - jnp.log(1.0 + jnp.exp(-jnp.abs(z)))


def _rms_scale(x):
    return lax.rsqrt(jnp.mean(x * x, axis=-1, keepdims=True) + EPS)


def _ada_kernel(c_ref, w_ref, b_ref, o_ref):
    c = c_ref[...]
    c_act = c * jax.nn.sigmoid(c)
    o_ref[...] = jnp.dot(c_act, w_ref[...], precision=HIGHEST, preferred_element_type=F32) + b_ref[...]


def _ada(c_pad, w, b, tn=1536):
    rows, d = c_pad.shape
    n = w.shape[1]
    return pl.pallas_call(
        _ada_kernel,
        grid=(n // tn,),
        in_specs=[pl.BlockSpec((rows, d), lambda j: (0, 0)),
                  pl.BlockSpec((d, tn), lambda j: (0, j)),
                  pl.BlockSpec((1, tn), lambda j: (0, j))],
        out_specs=pl.BlockSpec((rows, tn), lambda j: (0, j)),
        out_shape=jax.ShapeDtypeStruct((rows, n), F32),
        compiler_params=pltpu.CompilerParams(dimension_semantics=("arbitrary",), vmem_limit_bytes=VMEM_LIMIT),
        name="ada",
    )(c_pad, w, b)


def _inproj_kernel(x_ref, shift_ref, scale_ref, wm_ref, wlr_ref, wup_ref, bg_ref,
                   sbq_ref, sbk_ref, sbv_ref, gq_ref, gk_ref, gv_ref, gr_ref, la_ref):
    x = x_ref[0]
    h = (x * _rms_scale(x)) * (1.0 + scale_ref[0]) + shift_ref[0]
    hb = h.astype(BF16)
    sbw = SB_HEADS * SB_HEAD_DIM
    gkw = GLA_HEADS * GLA_DK
    gvw = GLA_HEADS * GLA_DV
    outs = ((sbq_ref, sbw, 0.125), (sbk_ref, sbw, None), (sbv_ref, sbw, None),
            (gq_ref, gkw, 0.125), (gk_ref, gkw, None), (gv_ref, gvw, None), (gr_ref, gvw, None))
    off = 0
    for ref, width, mul in outs:
        p = jnp.dot(hb, wm_ref[:, off:off + width], preferred_element_type=F32)
        if mul is not None:
            p = p * mul
        ref[0] = p.astype(ref.dtype)
        off += width
    g_lr = jnp.dot(hb, wlr_ref[...], preferred_element_type=F32)
    u = jnp.dot(g_lr, wup_ref[...], precision=HIGHEST, preferred_element_type=F32) + bg_ref[...]
    la_ref[0] = _log_sigmoid(u) * (1.0 / GLA_GATE_TAU)


def _inproj(x, shift, scale, w_main, w_lr, w_up, b_gate, tm=1024):
    b, s, d = x.shape
    sbw = SB_HEADS * SB_HEAD_DIM
    gkw = GLA_HEADS * GLA_DK
    gvw = GLA_HEADS * GLA_DV
    widths = (sbw, sbw, sbw, gkw, gkw, gvw, gvw)
    tok = lambda w: pl.BlockSpec((1, tm, w), lambda bi, i: (bi, i, 0))
    full = lambda a: pl.BlockSpec(a.shape, lambda bi, i: (0,) * a.ndim)
    mod = pl.BlockSpec((1, 1, d), lambda bi, i: (bi, 0, 0))
    return pl.pallas_call(
        _inproj_kernel,
        grid=(b, s // tm),
        in_specs=[tok(d), mod, mod, full(w_main), full(w_lr), full(w_up), full(b_gate)],
        out_specs=[tok(w) for w in widths] + [tok(gkw)],
        out_shape=[jax.ShapeDtypeStruct((b, s, w), BF16) for w in widths] + [jax.ShapeDtypeStruct((b, s, gkw), F32)],
        compiler_params=pltpu.CompilerParams(dimension_semantics=("arbitrary", "arbitrary"), vmem_limit_bytes=VMEM_LIMIT),
        name="inproj",
    )(x, shift, scale, w_main, w_lr, w_up, b_gate)


def _sb_kernel(q_ref, k_ref, v_ref, g_ref, o_ref, *, tq, nq):
    gi = pl.program_id(2)
    lane = lax.broadcasted_iota(jnp.int32, (tq, LANES), 1)
    row = lax.broadcasted_iota(jnp.int32, (tq, tq), 0)
    col = lax.broadcasted_iota(jnp.int32, (tq, tq), 1)
    later = jnp.where(row > col, 1.0, 0.0).astype(BF16)
    later2 = jnp.concatenate([later, later], axis=0)

    n_heads = LANES // SB_HEAD_DIM
    tqs = nq * tq
    rows = n_heads * tqs
    in_head = [(lane >= h * SB_HEAD_DIM) & (lane < (h + 1) * SB_HEAD_DIM) for h in range(n_heads)]
    nspan = SB_SPAN - 1 + nq
    span = nspan * tq
    q_row = lax.broadcasted_iota(jnp.int32, (rows, span), 0) & (tqs - 1)
    k_col = lax.broadcasted_iota(jnp.int32, (rows, span), 1)

    def suffix_in_blocks(log_keep, nblk):
        parts = []
        for b in range(nblk):
            lk = log_keep[:, b * tq:(b + 1) * tq]
            hi = lk.astype(BF16)
            parts.append(jnp.concatenate([hi, (lk - hi.astype(F32)).astype(BF16)], axis=1))
        cs = jnp.dot(jnp.concatenate(parts, axis=0), later2, preferred_element_type=F32)
        return [cs[b * rows:(b + 1) * rows] for b in range(nblk)]

    def first_span(q2, i):
        b0 = jnp.maximum(i - (SB_SPAN - 1), 0)
        start = pl.multiple_of(b0 * tq, tq)
        z = _nt(q2, k_ref[0, pl.ds(start, span), :])
        log_beta = _log_sigmoid(z)
        valid = k_col + (b0 - i) * tq < q_row
        log_keep = jnp.where(valid, log_beta - z, 0.0)
        within = suffix_in_blocks(log_keep, nspan)
        r_run = jnp.zeros((rows, 1), F32)
        ws = [None] * nspan
        for b in reversed(range(nspan)):
            blk = slice(b * tq, (b + 1) * tq)
            w = jnp.exp(log_beta[:, blk] + within[b] + r_run)
            ws[b] = jnp.where(valid[:, blk], w, 0.0).astype(BF16)
            r_run = r_run + jnp.sum(log_keep[:, blk], axis=-1, keepdims=True)
        acc = jnp.dot(jnp.concatenate(ws, axis=1), v_ref[0, pl.ds(start, span), :], preferred_element_type=F32)
        return b0 - 1, r_run, acc

    def one_block(q2, j, r_run, acc):
        start = pl.multiple_of(j * tq, tq)
        z = _nt(q2, k_ref[0, pl.ds(start, tq), :])
        log_beta = _log_sigmoid(z)
        log_keep = log_beta - z
        w = jnp.exp(log_beta + suffix_in_blocks(log_keep, 1)[0] + r_run)
        acc = acc + jnp.dot(w.astype(BF16), v_ref[0, pl.ds(start, tq), :], preferred_element_type=F32)
        return r_run + jnp.sum(log_keep, axis=-1, keepdims=True), acc

    q = q_ref[0]
    lane_q = lax.broadcasted_iota(jnp.int32, q.shape, 1)
    q2 = jnp.concatenate(
        [jnp.where((lane_q >= h * SB_HEAD_DIM) & (lane_q < (h + 1) * SB_HEAD_DIM), q, jnp.zeros_like(q))
         for h in range(n_heads)], axis=0)
    j0, r0, acc0 = first_span(q2, gi * nq)

    def cond(carry):
        j, r_run, _ = carry
        return jnp.logical_and(j >= 0, jnp.max(r_run) > SB_EXP_ZERO)

    def body(carry):
        j, r_run, acc = carry
        r_run, acc = one_block(q2, j, r_run, acc)
        return j - 1, r_run, acc

    _, _, acc = lax.while_loop(cond, body, (j0, r0, acc0))

    for qb in range(nq):
        rows_qb = slice(qb * tq, (qb + 1) * tq)
        o = jnp.where(in_head[0], acc[:tqs][rows_qb], acc[tqs:][rows_qb])
        o2 = o * o
        ms = jnp.zeros_like(o)
        for h in range(n_heads):
            ms_h = jnp.sum(jnp.where(in_head[h], o2, 0.0), axis=-1, keepdims=True) * (1.0 / SB_HEAD_DIM)
            ms = jnp.where(in_head[h], ms_h, ms)
        o_ref[0, rows_qb, :] = (o * lax.rsqrt(ms + EPS) * g_ref[...]).astype(o_ref.dtype)


def _sb_attention(q, k, v, gain, tq=128, nq=2):
    b, s, w = q.shape
    pairs = w // LANES
    return pl.pallas_call(
        functools.partial(_sb_kernel, tq=tq, nq=nq),
        grid=(b, pairs, s // (tq * nq)),
        in_specs=[pl.BlockSpec((1, tq * nq, LANES), lambda bi, p, i: (bi, i, p)),
                  pl.BlockSpec((1, s, LANES), lambda bi, p, i: (bi, 0, p)),
                  pl.BlockSpec((1, s, LANES), lambda bi, p, i: (bi, 0, p)),
                  pl.BlockSpec((1, LANES), lambda bi, p, i: (0, p))],
        out_specs=pl.BlockSpec((1, tq * nq, LANES), lambda bi, p, i: (bi, i, p)),
        out_shape=jax.ShapeDtypeStruct((b, s, w), BF16),
        compiler_params=pltpu.CompilerParams(dimension_semantics=("arbitrary",) * 3, vmem_limit_bytes=VMEM_LIMIT),
        name="sb",
    )(q, k, v, gain)


def _gla_kernel(q_ref, k_ref, v_ref, la_ref, r_ref, g_ref, o_ref, st_ref):
    c = GLA_CHUNK
    nsub = c // GLA_SUB
    n = GLA_GROUP * c
    w = la_ref.shape[2]

    @pl.when(pl.program_id(1) == 0)
    def _():
        st_ref[...] = jnp.zeros_like(st_ref)

    la = la_ref[0]
    row = lax.broadcasted_iota(jnp.int32, (n, n), 0)
    col = lax.broadcasted_iota(jnp.int32, (n, n), 1)
    incl = jnp.where(col <= row, 1.0, 0.0).astype(F32)
    g_cum = jnp.dot(incl, la, precision=HIGHEST, preferred_element_type=F32)
    g_exc = g_cum - la
    g_last = g_cum[n - 1:n, :]

    def rows_from(starts, count):
        return jnp.concatenate([jnp.broadcast_to(g_exc[s0:s0 + 1, :], (count, w)) for s0 in starts], axis=0)

    sub_ref = rows_from(range(0, n, GLA_SUB), GLA_SUB)
    chunk_ref = rows_from(range(0, n, c), c)
    e_q_sub = jnp.exp(g_cum - sub_ref)
    e_q_chunk = jnp.exp(g_cum - chunk_ref)
    e_inter = jnp.exp(g_cum)
    e_k_last = jnp.exp(g_last - g_cum)
    e_last = jnp.exp(g_last)
    e_k_sub = [jnp.exp(jnp.minimum(rows_from([c * ch + GLA_SUB * sb for ch in range(GLA_GROUP)], c) - g_cum, 0.0))
               for sb in range(1, nsub)]
    e_k_chunk = [jnp.exp(jnp.minimum(g_exc[c * ch:c * ch + 1, :] - g_cum, 0.0)) for ch in range(1, GLA_GROUP)]

    row_w = lax.broadcasted_iota(jnp.int32, (n, w), 0)
    seg_cum, seg_tot = la, la
    e_q_half, e_k_half = [], []
    half = 1
    while half < GLA_SUB:
        e_q_half.append(jnp.exp(seg_cum))
        e_k_half.append(jnp.exp(seg_tot - seg_cum))
        upper = (row_w & half) != 0
        partner_tot = jnp.where(upper, pltpu.roll(seg_tot, half, axis=0), pltpu.roll(seg_tot, n - half, axis=0))
        seg_cum = seg_cum + jnp.where(upper, partner_tot, 0.0)
        seg_tot = seg_tot + partner_tot
        half *= 2

    rowk = lax.broadcasted_iota(jnp.int32, (n, GLA_DK), 0)
    c_shift = c.bit_length() - 1
    s_shift = GLA_SUB.bit_length() - 1
    sub_of_row = (rowk & (c - 1)) >> s_shift
    chunk_of_row = rowk >> c_shift
    same_chunk = (row >> c_shift) == (col >> c_shift)
    earlier_sub = same_chunk & ((col >> s_shift) < (row >> s_shift))
    earlier = (col >> c_shift) < (row >> c_shift)
    half_masks = []
    for lvl in range(len(e_q_half)):
        half_masks.append(((row >> (lvl + 1)) == (col >> (lvl + 1))) & ((row & (1 << lvl)) != 0)
                          & ((col & (1 << lvl)) == 0))
    diagonal = row == col

    q_all = q_ref[0].astype(F32)
    k_all = k_ref[0].astype(F32)
    v_all = v_ref[0]
    gate = r_ref[0].astype(F32)
    gate = gate * jax.nn.sigmoid(gate)
    for h in range(GLA_HEADS):
        ks = slice(h * GLA_DK, (h + 1) * GLA_DK)
        vs = slice(h * GLA_DV, (h + 1) * GLA_DV)
        q = q_all[:, ks]
        k = k_all[:, ks]
        v = v_all[:, vs]
        scores = jnp.where(diagonal, _nt(q.astype(BF16), k.astype(BF16)), 0.0)
        for lvl in range(len(e_q_half)):
            bit = 1 << lvl
            q_up = jnp.where((rowk & bit) != 0, q * e_q_half[lvl][:, ks], 0.0).astype(BF16)
            k_lo = jnp.where((rowk & bit) == 0, k * e_k_half[lvl][:, ks], 0.0).astype(BF16)
            scores = jnp.where(half_masks[lvl], _nt(q_up, k_lo), scores)
        qt = q * e_q_sub[:, ks]
        q_hat = jnp.concatenate([jnp.where(sub_of_row == sb, qt, 0.0) for sb in range(1, nsub)], axis=1).astype(BF16)
        k_hat = jnp.concatenate(
            [jnp.where((rowk & (c - 1)) < GLA_SUB * sb, k * e_k_sub[sb - 1][:, ks], 0.0) for sb in range(1, nsub)],
            axis=1).astype(BF16)
        scores = jnp.where(earlier_sub, _nt(q_hat, k_hat), scores)
        if GLA_GROUP > 1:
            qc = q * e_q_chunk[:, ks]
            qc_hat = jnp.concatenate([jnp.where(chunk_of_row == ch, qc, 0.0) for ch in range(1, GLA_GROUP)],
                                     axis=1).astype(BF16)
            kc_hat = jnp.concatenate(
                [jnp.where(chunk_of_row < ch, k * e_k_chunk[ch - 1][:, ks], 0.0) for ch in range(1, GLA_GROUP)],
                axis=1).astype(BF16)
            scores = jnp.where(earlier, _nt(qc_hat, kc_hat), scores)
        o_intra = jnp.dot(scores.astype(BF16), v, preferred_element_type=F32)
        st = st_ref[h]
        o_inter = _nt((q * e_inter[:, ks]).astype(BF16), st.astype(BF16))
        k_dec = (k * e_k_last[:, ks]).astype(BF16)
        st_ref[h] = st * e_last[:, ks] + _tn(v, k_dec)
        o = o_intra + o_inter
        o = o * _rms_scale(o) * g_ref[:, vs] * gate[:, vs]
        o_ref[0, :, vs] = o.astype(o_ref.dtype)


def _gla(q, k, v, la, r, gain):
    b, s, _ = q.shape
    c = GLA_GROUP * GLA_CHUNK
    kw = GLA_HEADS * GLA_DK
    vw = GLA_HEADS * GLA_DV
    tok = lambda w: pl.BlockSpec((1, c, w), lambda bi, i: (bi, i, 0))
    return pl.pallas_call(
        _gla_kernel,
        grid=(b, s // c),
        in_specs=[tok(kw), tok(kw), tok(vw), tok(kw), tok(vw), pl.BlockSpec((1, vw), lambda bi, i: (0, 0))],
        out_specs=tok(vw),
        out_shape=jax.ShapeDtypeStruct((b, s, vw), BF16),
        scratch_shapes=[pltpu.VMEM((GLA_HEADS, GLA_DV, GLA_DK), F32)],
        compiler_params=pltpu.CompilerParams(dimension_semantics=("arbitrary", "arbitrary"), vmem_limit_bytes=VMEM_LIMIT),
        name="gla",
    )(q, k, v, la, r, gain)


def _mix_kernel(sb_ref, gla_ref, x_ref, gate_ref, shift_ref, scale_ref, wo_ref, wq_ref, ka_ref, kb_ref,
                x1_ref, h2_ref, st_ref):
    half = sb_ref.shape[2]
    mixed = (jnp.dot(sb_ref[0], wo_ref[:half, :], preferred_element_type=F32)
             + jnp.dot(gla_ref[0], wo_ref[half:, :], preferred_element_type=F32))
    x1 = x_ref[0] + gate_ref[0] * mixed
    x1_ref[0] = x1
    h2 = ((x1 * _rms_scale(x1)) * (1.0 + scale_ref[0]) + shift_ref[0]).astype(BF16)
    h2_ref[0] = h2
    q = jnp.dot(h2, wq_ref[...], preferred_element_type=F32).astype(BF16)
    for h in range(PEER_HEADS):
        base = 2 * PEER_HALF * h
        st_ref[2 * h] = _nt(ka_ref[h], q[:, base:base + PEER_HALF])
        st_ref[2 * h + 1] = _nt(kb_ref[h], q[:, base + PEER_HALF:base + 2 * PEER_HALF])


def _mix(sb, gla, x, gate, shift, scale, w_out, w_q, keys_a, keys_b, tm=512):
    b, s, d = x.shape
    ns = s // tm
    tok = lambda w: pl.BlockSpec((1, tm, w), lambda bi, i: (bi, i, 0))
    full = lambda a: pl.BlockSpec(a.shape, lambda bi, i: (0,) * a.ndim)
    mod = pl.BlockSpec((1, 1, d), lambda bi, i: (bi, 0, 0))
    return pl.pallas_call(
        _mix_kernel,
        grid=(b, ns),
        in_specs=[tok(sb.shape[2]), tok(gla.shape[2]), tok(d), mod, mod, mod,
                  full(w_out), full(w_q), full(keys_a), full(keys_b)],
        out_specs=[tok(d), tok(d),
                   pl.BlockSpec((2 * PEER_HEADS, PEER_KEYS, tm), lambda bi, i: (0, 0, bi * ns + i))],
        out_shape=[jax.ShapeDtypeStruct((b, s, d), F32), jax.ShapeDtypeStruct((b, s, d), BF16),
                   jax.ShapeDtypeStruct((2 * PEER_HEADS, PEER_KEYS, b * s), F32)],
        compiler_params=pltpu.CompilerParams(dimension_semantics=("arbitrary", "arbitrary"), vmem_limit_bytes=VMEM_LIMIT),
        name="mix",
    )(sb, gla, x, gate, shift, scale, w_out, w_q, keys_a, keys_b)


def _sort_network(n):
    pairs = []
    p = 1
    while p < n:
        k = p
        while k >= 1:
            for j in range(k % p, n - k, 2 * k):
                for i in range(min(k, n - j - k)):
                    if (i + j) // (2 * p) == (i + j + k) // (2 * p):
                        pairs.append((i + j, i + j + k))
            k //= 2
        p *= 2
    return pairs


SUBLANES = 8
_PAIR_COUNTS = tuple(PEER_TOPK // (a + 1) for a in range(PEER_TOPK))
_N_CAND = sum(_PAIR_COUNTS)
_N_CAND_PAD = -(-_N_CAND // SUBLANES) * SUBLANES


def _top_list(s, k):
    n = s.shape[0]
    depth = n // SUBLANES
    vals = [s[SUBLANES * v:SUBLANES * (v + 1), :] for v in range(depth)]
    base = lax.broadcasted_iota(jnp.int32, vals[0].shape, 0).astype(F32)
    idxs = [base + float(SUBLANES * v) for v in range(depth)]
    for lo, hi in _sort_network(depth):
        swap = (vals[hi] > vals[lo]) | ((vals[hi] == vals[lo]) & (idxs[hi] < idxs[lo]))
        vals[lo], vals[hi] = jnp.where(swap, vals[hi], vals[lo]), jnp.where(swap, vals[lo], vals[hi])
        idxs[lo], idxs[hi] = jnp.where(swap, idxs[hi], idxs[lo]), jnp.where(swap, idxs[lo], idxs[hi])
    top_v, top_i = [], []
    for step in range(k):
        m = jnp.max(vals[0], axis=0, keepdims=True)
        idx = jnp.min(jnp.where(vals[0] == m, idxs[0], float(n)), axis=0, keepdims=True)
        top_v.append(m)
        top_i.append(idx)
        hit = idxs[0] == idx
        live = min(depth, k - step)
        for p in range(live - 1):
            vals[p] = jnp.where(hit, vals[p + 1], vals[p])
            idxs[p] = jnp.where(hit, idxs[p + 1], idxs[p])
        if live == depth:
            vals[depth - 1] = jnp.where(hit, -jnp.inf, vals[depth - 1])
    return top_v, top_i


def _topk_kernel(s_ref, i_ref, j_ref, g_ref, cand_ref, code_ref, oc_ref, og_ref):
    tl = s_ref.shape[2]
    k = PEER_TOPK
    iota_cand = lax.broadcasted_iota(jnp.int32, (_N_CAND_PAD, tl), 0).astype(F32)

    def head(h, carry):
        top_v, top_i = _top_list(jnp.concatenate([s_ref[2 * h], s_ref[2 * h + 1]], axis=1), k)
        v1 = [v[:, :tl] for v in top_v]
        i1 = [i[:, :tl] for i in top_i]
        v2_all = jnp.concatenate([v[:, tl:] for v in top_v], axis=0)
        i2_all = jnp.concatenate([i[:, tl:] for i in top_i], axis=0)
        cand_ref[...] = jnp.full((_N_CAND_PAD, tl), -jnp.inf, F32)
        code_ref[...] = jnp.zeros((_N_CAND_PAD, tl), F32)
        off = 0
        for a, nb in enumerate(_PAIR_COUNTS):
            cand_ref[off:off + nb, :] = v1[a] + v2_all[:nb]
            code_ref[off:off + nb, :] = i1[a] * float(PEER_KEYS) + i2_all[:nb]
            off += nb
        cand = cand_ref[...]
        code = code_ref[...]
        tops, sel = [], []
        for _ in range(k):
            m = jnp.max(cand, axis=0, keepdims=True)
            pos = jnp.min(jnp.where(cand == m, iota_cand, float(_N_CAND_PAD)), axis=0, keepdims=True)
            hit = iota_cand == pos
            tops.append(m)
            sel.append(jnp.sum(jnp.where(hit, code, 0.0), axis=0, keepdims=True))
            cand = jnp.where(hit, -jnp.inf, cand)
        top = jnp.concatenate(tops, axis=0)
        e = jnp.exp(top - top[0:1])
        gates = e / jnp.sum(e, axis=0, keepdims=True)
        rows = pl.ds(pl.multiple_of(h * k, k), k)
        oc_ref[rows, :] = jnp.concatenate(sel, axis=0).astype(jnp.int32)
        og_ref[rows, :] = gates
        return carry

    lax.fori_loop(0, PEER_HEADS, head, 0)
    shift = PEER_KEYS.bit_length() - 1
    for t0 in range(0, tl, LANES):
        code = oc_ref[:, t0:t0 + LANES].T
        i_ref[t0:t0 + LANES, :] = code >> shift
        j_ref[t0:t0 + LANES, :] = code & (PEER_KEYS - 1)
        g_ref[t0:t0 + LANES, :] = og_ref[:, t0:t0 + LANES].T


def _topk(scores_t, tl=512):
    nh2, nk, t = scores_t.shape
    nsel = PEER_HEADS * PEER_TOPK
    out = pl.BlockSpec((tl, nsel), lambda i: (i, 0))
    return pl.pallas_call(
        _topk_kernel,
        grid=(t // tl,),
        in_specs=[pl.BlockSpec((nh2, nk, tl), lambda i: (0, 0, i))],
        out_specs=[out, out, out],
        out_shape=[jax.ShapeDtypeStruct((t, nsel), jnp.int32), jax.ShapeDtypeStruct((t, nsel), jnp.int32),
                   jax.ShapeDtypeStruct((t, nsel), F32)],
        scratch_shapes=[pltpu.VMEM((_N_CAND_PAD, tl), F32), pltpu.VMEM((_N_CAND_PAD, tl), F32),
                        pltpu.VMEM((nsel, tl), jnp.int32), pltpu.VMEM((nsel, tl), F32)],
        compiler_params=pltpu.CompilerParams(dimension_semantics=("arbitrary",), vmem_limit_bytes=VMEM_LIMIT),
        name="topk",
    )(scores_t)


def _peer_kernel(h2_ref, u_ref, v_ref, isel_ref, jsel_ref, gsel_ref, x1_ref, gate_ref, gf_ref, o_ref,
                 gs_ref, *, tm, te):
    e = pl.program_id(1)
    n_i = te // PEER_KEYS

    @pl.when(e == 0)
    def _():
        o_ref[...] = jnp.zeros_like(o_ref)
        iota0 = lax.broadcasted_iota(jnp.int32, (PEER_KEYS, LANES), 0)

        def per_token(t, carry):
            isel = isel_ref[pl.ds(t, 1), :]
            jsel = jsel_ref[pl.ds(t, 1), :]
            g = gsel_ref[pl.ds(t, 1), :]
            gi = jnp.where(iota0 == isel, g, 0.0).astype(BF16)
            oj = jnp.where(iota0 == jsel, 1.0, 0.0).astype(BF16)
            gs_ref[pl.ds(t * GS_PITCH, PEER_KEYS), :] = _nt(gi, oj)
            return carry

        lax.fori_loop(0, tm, per_token, 0, unroll=64)

    a = _nt(h2_ref[...], u_ref[...])
    act = 0.5 * a * (1.0 + lax.erf(a * (1.0 / math.sqrt(2.0))))
    parts = []
    for ii in range(n_i):
        gm = gs_ref[pl.ds(e * n_i + ii, tm, stride=GS_PITCH), :]
        parts.append((act[:, ii * PEER_KEYS:(ii + 1) * PEER_KEYS] * gm).astype(BF16))
    w = jnp.concatenate(parts, axis=1)
    o_ref[...] += jnp.dot(w, v_ref[...], preferred_element_type=F32)

    @pl.when(e == pl.num_programs(1) - 1)
    def _():
        x2 = x1_ref[...] + gate_ref[0] * o_ref[...]
        o_ref[...] = x2 * _rms_scale(x2) * gf_ref[...]


def _peer(h2, u, v, isel, jsel, gsel, x1, gate, g_final, seq, tm=512, te=1024):
    t, d = h2.shape
    n_e = u.shape[0]
    per_batch = seq // tm
    tok = lambda w: pl.BlockSpec((tm, w), lambda i, e: (i, 0))
    exp = pl.BlockSpec((te, d), lambda i, e: (e, 0))
    return pl.pallas_call(
        functools.partial(_peer_kernel, tm=tm, te=te),
        grid=(t // tm, n_e // te),
        in_specs=[tok(d), exp, exp, tok(isel.shape[1]), tok(isel.shape[1]), tok(isel.shape[1]),
                  pl.BlockSpec((tm, d), lambda i, e: (i, 0), pipeline_mode=pl.Buffered(1)),
                  pl.BlockSpec((1, 1, d), lambda i, e: (i // per_batch, 0, 0)),
                  pl.BlockSpec((1, d), lambda i, e: (0, 0))],
        out_specs=tok(d),
        out_shape=jax.ShapeDtypeStruct((t, d), F32),
        scratch_shapes=[pltpu.VMEM((tm * GS_PITCH, LANES), F32)],
        compiler_params=pltpu.CompilerParams(dimension_semantics=("arbitrary", "arbitrary"), vmem_limit_bytes=VMEM_LIMIT),
        name="peer",
    )(h2, u, v, isel, jsel, gsel, x1, gate, g_final)


def kernel(x, c, w_ada, b_ada, w_in, w_gla_gate_up, b_gla_gate, g_sb_norm, g_gla_norm, w_out, w_peer_q,
           peer_keys_a, peer_keys_b, peer_u, peer_v, g_final):
    b, s, d = x.shape
    depth = w_ada.shape[0]
    sbw = SB_HEADS * SB_HEAD_DIM
    gkw = GLA_HEADS * GLA_DK
    gvw = GLA_HEADS * GLA_DV
    lr0 = 3 * sbw + 2 * gkw + gvw
    c_pad = jnp.pad(c, ((0, 8 - b), (0, 0)))
    for l in range(depth):
        mod = _ada(c_pad, w_ada[l], b_ada[l][None, :])[:b]
        shift1, scale1, gate1, shift2, scale2, gate2 = (mod[:, None, i * d:(i + 1) * d] for i in range(6))

        w_l = w_in[l]
        w_main = jnp.concatenate([w_l[:, :lr0], w_l[:, lr0 + GLA_GATE_RANK:]], axis=1).astype(BF16)
        w_lr = jnp.pad(w_l[:, lr0:lr0 + GLA_GATE_RANK], ((0, 0), (0, LANES - GLA_GATE_RANK))).astype(BF16)
        w_up = jnp.pad(w_gla_gate_up[l], ((0, LANES - GLA_GATE_RANK), (0, 0)))
        sbq, sbk, sbv, gq, gk, gv, gr, la = _inproj(x, shift1, scale1, w_main, w_lr, w_up, b_gla_gate[l][None, :])

        sb_out = _sb_attention(sbq, sbk, sbv, g_sb_norm[l][None, :])
        gla_out = _gla(gq, gk, gv, la, gr, g_gla_norm[l][None, :])

        x1, h2, scores_t = _mix(sb_out, gla_out, x, gate1, shift2, scale2, w_out[l].astype(BF16),
                                w_peer_q[l].astype(BF16), peer_keys_a[l].astype(BF16), peer_keys_b[l].astype(BF16))
        isel, jsel, gsel = _topk(scores_t)
        x = _peer(h2.reshape(b * s, d), peer_u[l].astype(BF16), peer_v[l].astype(BF16), isel, jsel, gsel,
                  x1.reshape(b * s, d), gate2, g_final[None, :], s).reshape(b, s, d)
    return x
```
